```python
import jax, jax.numpy as jnp
from jax import lax
import numpy as np

D_MODEL = 1024
BATCH = 1
SEQ = 16384
DEPTH = 4

CHUNK = 64
MEM_LEN = 256
N_A_LAYERS = DEPTH // 2
N_B_LAYERS = DEPTH - N_A_LAYERS
MEM_HEADS = 4
MEM_HEAD_DIM = 64
MEM_W = MEM_HEADS * MEM_HEAD_DIM
MAIN_W = D_MODEL - MEM_W
CONV_W = MAIN_W
CONV_K = 3
DIFF_HEAD_DIM = 64
DIFF_HEADS = MAIN_W // (2 * DIFF_HEAD_DIM)
DIFF_QK = DIFF_HEADS * DIFF_HEAD_DIM
ROPE_DIM = DIFF_HEAD_DIM // 4
ROPE_THETA = 500000.0
D_FF = 4 * D_MODEL
Q_BLOCK = 128
EPS = 1e-6
SUBLN_EPS = 1e-5

kernel_name = "yoco_shortconv_diffattn_memory_trunk"


def rmsnorm(x, g, eps=EPS):
    xf = x.astype(jnp.float32)
    y = xf * lax.rsqrt(jnp.mean(xf * xf, axis=-1, keepdims=True) + eps)
    return (y * g.astype(jnp.float32)).astype(x.dtype)


def rope_tables(seq):
    inv = 1.0 / (ROPE_THETA ** (jnp.arange(0, ROPE_DIM, 2, dtype=jnp.float32) / ROPE_DIM))
    ang = jnp.arange(seq, dtype=jnp.float32)[:, None] * inv[None, :]
    return jnp.cos(ang), jnp.sin(ang)


def partial_rope(x, cos, sin):
    half = ROPE_DIM // 2
    c = cos[None, :, None, :].astype(x.dtype)
    s = sin[None, :, None, :].astype(x.dtype)
    x1, x2, xp = x[..., :half], x[..., half:ROPE_DIM], x[..., ROPE_DIM:]
    return jnp.concatenate([x1 * c - x2 * s, x1 * s + x2 * c, xp], axis=-1)


def short_conv_mixer(h, gate_b, gate_c, w_conv):
    u = gate_c * h
    s_len = u.shape[1]
    up = jnp.pad(u, ((0, 0), (CONV_K - 1, 0), (0, 0)))
    y = w_conv[0] * up[:, 0:s_len]
    for k in range(1, CONV_K):
        y = y + w_conv[k] * up[:, k:k + s_len]
    return gate_b * y


def memory_attention(q, mem_n, w_mem_kv, q_gain, k_gain):
    b, s, _ = q.shape
    m = mem_n.shape[1]
    kv = jnp.einsum('bmd,de->bme', mem_n, w_mem_kv)
    k = kv[..., :MEM_W].reshape(b, m, MEM_HEADS, MEM_HEAD_DIM)
    v = kv[..., MEM_W:].reshape(b, m, MEM_HEADS, MEM_HEAD_DIM)
    qh = rmsnorm(q.reshape(b, s, MEM_HEADS, MEM_HEAD_DIM), q_gain)
    k = rmsnorm(k, k_gain)
    sc = jnp.einsum('bshd,bmhd->bhsm', qh, k).astype(jnp.float32) * (MEM_HEAD_DIM ** -0.5)
    p = jax.nn.softmax(sc, axis=-1).astype(v.dtype)
    o = jnp.einsum('bhsm,bmhd->bshd', p, v)
    return o.reshape(b, s, MEM_W)


def diff_attention(q1, q2, k1, k2, v, lam):
    b, s, h, dh = q1.shape
    nblk = s // Q_BLOCK
    scale = dh ** -0.5
    key_chunk = jnp.arange(s) // CHUNK
    qb1 = jnp.moveaxis(q1.reshape(b, nblk, Q_BLOCK, h, dh), 1, 0)
    qb2 = jnp.moveaxis(q2.reshape(b, nblk, Q_BLOCK, h, dh), 1, 0)

    def one_block(args):
        a1, a2, i = args
        q_chunk = (i * Q_BLOCK + jnp.arange(Q_BLOCK)) // CHUNK
        mask = (key_chunk[None, :] <= q_chunk[:, None])[None, None]
        s1 = jnp.einsum('bqhd,bkhd->bhqk', a1, k1).astype(jnp.float32) * scale
        s2 = jnp.einsum('bqhd,bkhd->bhqk', a2, k2).astype(jnp.float32) * scale
        p1 = jax.nn.softmax(jnp.where(mask, s1, -jnp.inf), axis=-1)
        p2 = jax.nn.softmax(jnp.where(mask, s2, -jnp.inf), axis=-1)
        p = (p1 - lam * p2).astype(v.dtype)
        return jnp.einsum('bhqk,bkhe->bqhe', p, v)

    out = lax.map(one_block, (qb1, qb2, jnp.arange(nblk)))
    return jnp.moveaxis(out, 0, 1).reshape(b, s, h, 2 * dh)


def sqrelu_mlp(h, w_up, w_down):
    u = jnp.einsum('bsd,df->bsf', h, w_up)
    return jnp.einsum('bsf,fd->bsd', jnp.square(jax.nn.relu(u)), w_down)


def setup_inputs(seed: int = 0) -> dict:
    key = jax.random.key(seed)
    ks = jax.random.split(key, 24)
    f32 = jnp.float32
    nrm = lambda k, shp, sc: jax.random.normal(k, shp, f32) * sc
    gain = lambda k, shp: 1.0 + 0.02 * jax.random.normal(k, shp, f32)
    return {
        "x": nrm(ks[0], (BATCH, SEQ, D_MODEL), 1.0),
        "mem": nrm(ks[1], (BATCH, MEM_LEN, D_MODEL), 1.0),
        "norm_mix": gain(ks[2], (DEPTH, D_MODEL)),
        "norm_mlp": gain(ks[3], (DEPTH, D_MODEL)),
        "a_w_in": nrm(ks[4], (N_A_LAYERS, D_MODEL, 3 * CONV_W + MEM_W), D_MODEL ** -0.5),
        "a_conv": nrm(ks[5], (N_A_LAYERS, CONV_K, CONV_W), CONV_K ** -0.5),
        "b_w_q": nrm(ks[6], (N_B_LAYERS, D_MODEL, 2 * DIFF_QK + MEM_W), D_MODEL ** -0.5),
        "b_q_norm": gain(ks[7], (N_B_LAYERS, DIFF_HEAD_DIM)),
        "b_lam": nrm(ks[8], (N_B_LAYERS, 4, DIFF_HEAD_DIM), 0.1),
        "b_subln": gain(ks[9], (N_B_LAYERS, 2 * DIFF_HEAD_DIM)),
        "kv_norm": gain(ks[10], (D_MODEL,)),
        "w_kv": nrm(ks[11], (D_MODEL, 2 * DIFF_QK + MAIN_W), D_MODEL ** -0.5),
        "k_norm": gain(ks[12], (DIFF_HEAD_DIM,)),
        "mem_norm": gain(ks[13], (D_MODEL,)),
        "w_mem_kv": nrm(ks[14], (DEPTH, D_MODEL, 2 * MEM_W), D_MODEL ** -0.5),
        "mem_q_norm": gain(ks[15], (DEPTH, MEM_HEAD_DIM)),
        "mem_k_norm": gain(ks[16], (DEPTH, MEM_HEAD_DIM)),
        "w_o": nrm(ks[17], (DEPTH, MAIN_W + MEM_W, D_MODEL), (MAIN_W + MEM_W) ** -0.5),
        "w_up": nrm(ks[18], (DEPTH, D_MODEL, D_FF), D_MODEL ** -0.5),
        "w_down": nrm(ks[19], (DEPTH, D_FF, D_MODEL), 0.5 * D_FF ** -0.5),
    }


def reference(x, mem, norm_mix, norm_mlp, a_w_in, a_conv, b_w_q, b_q_norm, b_lam, b_subln,
              kv_norm, w_kv, k_norm, mem_norm, w_mem_kv, mem_q_norm, mem_k_norm,
              w_o, w_up, w_down):
    b, s, _ = x.shape
    cos, sin = rope_tables(s)
    mem_n = rmsnorm(mem, mem_norm)
    k1 = k2 = v = None
    for l in range(DEPTH):
        h = rmsnorm(x, norm_mix[l])
        if l < N_A_LAYERS:
            proj = jnp.einsum('bsd,de->bse', h, a_w_in[l])
            gate_b = proj[..., :CONV_W]
            gate_c = proj[..., CONV_W:2 * CONV_W]
            hv = proj[..., 2 * CONV_W:3 * CONV_W]
            qm = proj[..., 3 * CONV_W:]
            main = short_conv_mixer(hv, gate_b, gate_c, a_conv[l])
        else:
            j = l - N_A_LAYERS
            proj = jnp.einsum('bsd,de->bse', h, b_w_q[j])
            q1 = proj[..., :DIFF_QK].reshape(b, s, DIFF_HEADS, DIFF_HEAD_DIM)
            q2 = proj[..., DIFF_QK:2 * DIFF_QK].reshape(b, s, DIFF_HEADS, DIFF_HEAD_DIM)
            qm = proj[..., 2 * DIFF_QK:]
            q1 = partial_rope(rmsnorm(q1, b_q_norm[j]), cos, sin)
            q2 = partial_rope(rmsnorm(q2, b_q_norm[j]), cos, sin)
            lam_init = 0.8 - 0.6 * float(np.exp(-0.3 * l))
            lp = b_lam[j].astype(jnp.float32)
            lam = (jnp.exp(jnp.sum(lp[0] * lp[1])) - jnp.exp(jnp.sum(lp[2] * lp[3]))
                   + lam_init)
            o = diff_attention(q1, q2, k1, k2, v, lam)
            o = rmsnorm(o, b_subln[j], SUBLN_EPS) * (1.0 - lam_init)
            main = o.reshape(b, s, MAIN_W)
        mo = memory_attention(qm, mem_n, w_mem_kv[l], mem_q_norm[l], mem_k_norm[l])
        x = x + jnp.einsum('bse,ed->bsd', jnp.concatenate([main, mo], axis=-1), w_o[l])
        x = x + sqrelu_mlp(rmsnorm(x, norm_mlp[l]), w_up[l], w_down[l])
        if l == N_A_LAYERS - 1:
            kvh = rmsnorm(x, kv_norm)
            kv = jnp.einsum('bsd,de->bse', kvh, w_kv)
            k1 = kv[..., :DIFF_QK].reshape(b, s, DIFF_HEADS, DIFF_HEAD_DIM)
            k2 = kv[..., DIFF_QK:2 * DIFF_QK].reshape(b, s, DIFF_HEADS, DIFF_HEAD_DIM)
            v = kv[..., 2 * DIFF_QK:].reshape(b, s, DIFF_HEADS, 2 * DIFF_HEAD_DIM)
            k1 = partial_rope(rmsnorm(k1, k_norm), cos, sin)
            k2 = partial_rope(rmsnorm(k2, k_norm), cos, sin)
    return x
```

```python
import functools
import math

import jax
import jax.numpy as jnp
from jax import lax
from jax.experimental import pallas as pl
from jax.experimental.pallas import tpu as pltpu

F32 = jnp.float32
BF16 = jnp.bfloat16

LANES = 128
HEAD_DIM = 64
CHUNK = 64
ROPE_DIM = HEAD_DIM // 4
ROPE_THETA = 500000.0
CONV_K = 3
EPS = 1e-6
SUBLN_EPS = 1e-5
ROW_TILE = 512
ATTN_TILE = 512
FF_CHUNK = 1024
VMEM_LIMIT = 56 * 1024 * 1024


def _resident(shape):
    nd = len(shape)
    return pl.BlockSpec(shape, lambda *_: (0,) * nd, pipeline_mode=pl.Buffered(1))


def _rms(x, g, eps=EPS):
    return x * lax.rsqrt(jnp.mean(x * x, axis=-1, keepdims=True) + eps) * g


def _group_matrix():
    r = lax.broadcasted_iota(jnp.int32, (LANES, LANES), 0) // HEAD_DIM
    c = lax.broadcasted_iota(jnp.int32, (LANES, LANES), 1) // HEAD_DIM
    return jnp.where(r == c, 1.0 / HEAD_DIM, 0.0).astype(BF16)


def _head_rms(x, gain, gmat):
    ms = jnp.dot((x * x).astype(BF16), gmat, preferred_element_type=F32)
    return x * lax.rsqrt(ms + EPS) * gain


def _rope(x, cos_t, sin_lo, sin_hi):
    return (x * cos_t + pltpu.roll(x, LANES - ROPE_DIM // 2, 1) * sin_lo
            + pltpu.roll(x, ROPE_DIM // 2, 1) * sin_hi)


def _mem_attention(qm, qgain, mk_ref, mv_ref, gmat):
    outs = []
    for g in range(qm.shape[1] // LANES):
        sl = slice(g * LANES, (g + 1) * LANES)
        qn = _head_rms(qm[:, sl], qgain[:, sl], gmat)
        lane = lax.broadcasted_iota(jnp.int32, qn.shape, 1)
        kg = mk_ref[:, sl]
        ps = []
        for hh in range(LANES // HEAD_DIM):
            keep = (lane < HEAD_DIM) if hh == 0 else (lane >= HEAD_DIM)
            qh = jnp.where(keep, qn, 0.0).astype(BF16)
            s = lax.dot_general(qh, kg, (((1,), (1,)), ((), ())), preferred_element_type=F32)
            e = jnp.exp(s - jnp.max(s, axis=-1, keepdims=True))
            p = e * (1.0 / jnp.sum(e, axis=-1, keepdims=True))
            ps.append(p.astype(BF16))
        outs.append(jnp.dot(jnp.concatenate(ps, axis=1), mv_ref[g], preferred_element_type=F32))
    return jnp.concatenate(outs, axis=1)


def _mem_prep_kernel(mem_ref, mnorm_ref, w_ref, kgain_ref, mk_ref, mv_ref):
    gmat = _group_matrix()
    mem_n = _rms(mem_ref[...], mnorm_ref[...]).astype(BF16)
    kv = jnp.dot(mem_n, w_ref[0].astype(BF16), preferred_element_type=F32)
    mem_w = kv.shape[1] // 2
    scale = HEAD_DIM ** -0.5
    for g in range(mem_w // LANES):
        sl = slice(g * LANES, (g + 1) * LANES)
        kn = _head_rms(kv[:, sl], kgain_ref[0][:, sl], gmat) * scale
        mk_ref[0, :, sl] = kn.astype(BF16)
        v = kv[:, mem_w + g * LANES: mem_w + (g + 1) * LANES]
        lane = lax.broadcasted_iota(jnp.int32, v.shape, 1)
        m = v.shape[0]
        mv_ref[0, g, 0:m, :] = jnp.where(lane < HEAD_DIM, v, 0.0).astype(BF16)
        mv_ref[0, g, m:2 * m, :] = jnp.where(lane >= HEAD_DIM, v, 0.0).astype(BF16)


def _mem_prep(mem, mem_norm, w_mem_kv, k_gain):
    depth, d, two_w = w_mem_kv.shape
    m = mem.shape[0]
    mem_w = two_w // 2
    groups = mem_w // LANES
    return pl.pallas_call(
        _mem_prep_kernel,
        grid=(depth,),
        in_specs=[
            pl.BlockSpec((m, d), lambda l: (0, 0)),
            pl.BlockSpec((1, d), lambda l: (0, 0)),
            pl.BlockSpec((1, d, two_w), lambda l: (l, 0, 0)),
            pl.BlockSpec((1, 1, mem_w), lambda l: (l, 0, 0)),
        ],
        out_specs=[
            pl.BlockSpec((1, m, mem_w), lambda l: (l, 0, 0)),
            pl.BlockSpec((1, groups, 2 * m, LANES), lambda l: (l, 0, 0, 0)),
        ],
        out_shape=[
            jax.ShapeDtypeStruct((depth, m, mem_w), BF16),
            jax.ShapeDtypeStruct((depth, groups, 2 * m, LANES), BF16),
        ],
        compiler_params=pltpu.CompilerParams(dimension_semantics=("arbitrary",)),
        name="mem_prep",
    )(mem, mem_norm, w_mem_kv, k_gain)


def _a_mixer_kernel(x_ref, g_ref, win_ref, conv_ref, qgain_ref, mk_ref, mv_ref, wo_ref,
                    o_ref, u_scr, *, conv_w):
    tm = x_ref.shape[0]
    halo = 8

    @pl.when(pl.program_id(0) == 0)
    def _():
        u_scr[0:halo, :] = jnp.zeros((halo, conv_w), F32)

    x = x_ref[...]
    h = _rms(x, g_ref[...]).astype(BF16)
    proj = jnp.dot(h, win_ref[...], preferred_element_type=F32)
    gate_b = proj[:, 0:conv_w]
    u = proj[:, conv_w:2 * conv_w] * proj[:, 2 * conv_w:3 * conv_w]
    qm = proj[:, 3 * conv_w:]

    u_scr[halo:halo + tm, :] = u
    w = conv_ref[...]
    y = (w[0:1, :] * u_scr[halo - 2:halo - 2 + tm, :]
         + w[1:2, :] * u_scr[halo - 1:halo - 1 + tm, :]
         + w[2:3, :] * u)
    u_scr[0:halo, :] = u[tm - halo:tm, :]
    main = (gate_b * y).astype(BF16)

    mo = _mem_attention(qm, qgain_ref[...], mk_ref, mv_ref, _group_matrix()).astype(BF16)
    o_ref[...] = (x
                  + jnp.dot(main, wo_ref[0:conv_w, :], preferred_element_type=F32)
                  + jnp.dot(mo, wo_ref[conv_w:, :], preferred_element_type=F32))


def _a_mixer(x, g, w_in, conv, qgain, mk, mv, w_o):
    s, d = x.shape
    conv_w = conv.shape[1]
    tm = ROW_TILE
    return pl.pallas_call(
        functools.partial(_a_mixer_kernel, conv_w=conv_w),
        grid=(s // tm,),
        in_specs=[
            pl.BlockSpec((tm, d), lambda i: (i, 0)),
            _resident(g.shape), _resident(w_in.shape), _resident(conv.shape),
            _resident(qgain.shape), _resident(mk.shape), _resident(mv.shape),
            _resident(w_o.shape),
        ],
        out_specs=pl.BlockSpec((tm, d), lambda i: (i, 0)),
        out_shape=jax.ShapeDtypeStruct((s, d), F32),
        scratch_shapes=[pltpu.VMEM((tm + 8, conv_w), F32)],
        compiler_params=pltpu.CompilerParams(dimension_semantics=("arbitrary",),
                                             vmem_limit_bytes=VMEM_LIMIT),
        name="a_mixer",
    )(x, g, w_in, conv, qgain, mk, mv, w_o)


def _mlp_kernel(x_ref, g_ref, wup_ref, wdown_ref, o_ref):
    x = x_ref[...]
    h = _rms(x, g_ref[...]).astype(BF16)
    acc = x
    for c in range(wup_ref.shape[1] // FF_CHUNK):
        sl = slice(c * FF_CHUNK, (c + 1) * FF_CHUNK)
        a = jnp.maximum(jnp.dot(h, wup_ref[:, sl], preferred_element_type=F32), 0.0)
        acc = acc + jnp.dot((a * a).astype(BF16), wdown_ref[sl, :], preferred_element_type=F32)
    o_ref[...] = acc


def _mlp(x, g, w_up, w_down):
    s, d = x.shape
    tm = ROW_TILE
    return pl.pallas_call(
        _mlp_kernel,
        grid=(s // tm,),
        in_specs=[pl.BlockSpec((tm, d), lambda i: (i, 0)),
                  _resident(g.shape), _resident(w_up.shape), _resident(w_down.shape)],
        out_specs=pl.BlockSpec((tm, d), lambda i: (i, 0)),
        out_shape=jax.ShapeDtypeStruct((s, d), F32),
        compiler_params=pltpu.CompilerParams(dimension_semantics=("arbitrary",),
                                             vmem_limit_bytes=VMEM_LIMIT),
        name="mlp",
    )(x, g, w_up, w_down)


def _kv_proj_kernel(x_ref, g_ref, w_ref, kgain_ref, rope_ref, kk_ref, vv_ref, *, qk_w):
    gmat = _group_matrix()
    h = _rms(x_ref[...], g_ref[...]).astype(BF16)
    kv = jnp.dot(h, w_ref[...], preferred_element_type=F32)
    cos_t, sin_lo, sin_hi = rope_ref[0], rope_ref[1], rope_ref[2]
    for g in range(qk_w // LANES):
        sl = slice(g * LANES, (g + 1) * LANES)
        kn = _head_rms(kv[:, sl], kgain_ref[...], gmat)
        kk_ref[:, sl] = _rope(kn, cos_t, sin_lo, sin_hi).astype(BF16)
    vv_ref[...] = kv[:, qk_w:].astype(BF16)


def _kv_proj(x, g, w_kv, kgain, rope, qk_w):
    s, d = x.shape
    tm = ROW_TILE
    v_w = w_kv.shape[1] - qk_w
    return pl.pallas_call(
        functools.partial(_kv_proj_kernel, qk_w=qk_w),
        grid=(s // tm,),
        in_specs=[pl.BlockSpec((tm, d), lambda i: (i, 0)),
                  _resident(g.shape), _resident(w_kv.shape), _resident(kgain.shape),
                  pl.BlockSpec((3, tm, LANES), lambda i: (0, i, 0))],
        out_specs=[pl.BlockSpec((tm, qk_w), lambda i: (i, 0)),
                   pl.BlockSpec((tm, v_w), lambda i: (i, 0))],
        out_shape=[jax.ShapeDtypeStruct((s, qk_w), BF16), jax.ShapeDtypeStruct((s, v_w), BF16)],
        compiler_params=pltpu.CompilerParams(dimension_semantics=("arbitrary",),
                                             vmem_limit_bytes=VMEM_LIMIT),
        name="kv_proj",
    )(x, g, w_kv, kgain, rope)


def _b_qproj_kernel(x_ref, g_ref, w_ref, qgain_ref, rope_ref, qq_ref, qm_ref, *, qk_w):
    gmat = _group_matrix()
    h = _rms(x_ref[...], g_ref[...]).astype(BF16)
    proj = jnp.dot(h, w_ref[...], preferred_element_type=F32)
    cos_t, sin_lo, sin_hi = rope_ref[0], rope_ref[1], rope_ref[2]
    scale = HEAD_DIM ** -0.5
    for g in range(qk_w // LANES):
        sl = slice(g * LANES, (g + 1) * LANES)
        qn = _head_rms(proj[:, sl], qgain_ref[...], gmat)
        qq_ref[:, sl] = (_rope(qn, cos_t, sin_lo, sin_hi) * scale).astype(BF16)
    qm_ref[...] = proj[:, qk_w:]


def _b_qproj(x, g, w_q, qgain, rope, qk_w):
    s, d = x.shape
    tm = ROW_TILE
    mem_w = w_q.shape[1] - qk_w
    return pl.pallas_call(
        functools.partial(_b_qproj_kernel, qk_w=qk_w),
        grid=(s // tm,),
        in_specs=[pl.BlockSpec((tm, d), lambda i: (i, 0)),
                  _resident(g.shape), _resident(w_q.shape), _resident(qgain.shape),
                  pl.BlockSpec((3, tm, LANES), lambda i: (0, i, 0))],
        out_specs=[pl.BlockSpec((tm, qk_w), lambda i: (i, 0)),
                   pl.BlockSpec((tm, mem_w), lambda i: (i, 0))],
        out_shape=[jax.ShapeDtypeStruct((s, qk_w), BF16), jax.ShapeDtypeStruct((s, mem_w), F32)],
        compiler_params=pltpu.CompilerParams(dimension_semantics=("arbitrary",),
                                             vmem_limit_bytes=VMEM_LIMIT),
        name="b_qproj",
    )(x, g, w_q, qgain, rope)


def _diff_attn_kernel(lam_ref, subln_ref, qq_ref, kk_ref, vv_ref, o_ref,
                      q2_scr, m_scr, l_scr, acc_scr, *, lam_init):
    t = qq_ref.shape[0]
    qi = pl.program_id(1)
    q = qq_ref[...]
    lane = lax.broadcasted_iota(jnp.int32, q.shape, 1)
    q2_scr[0:t, :] = jnp.where(lane < HEAD_DIM, q, jnp.zeros_like(q))
    q2_scr[t:2 * t, :] = jnp.where(lane >= HEAD_DIM, q, jnp.zeros_like(q))
    m_scr[...] = jnp.full(m_scr.shape, -jnp.inf, F32)
    l_scr[...] = jnp.zeros(l_scr.shape, F32)
    acc_scr[...] = jnp.zeros(acc_scr.shape, F32)

    def step(kstart, diagonal):
        k = kk_ref[pl.ds(kstart, t), :]
        v = vv_ref[pl.ds(kstart, t), :]
        s = lax.dot_general(q2_scr[...], k, (((1,), (1,)), ((), ())),
                            preferred_element_type=F32)
        if diagonal:
            row = lax.broadcasted_iota(jnp.int32, s.shape, 0)
            col = lax.broadcasted_iota(jnp.int32, s.shape, 1)
            s = jnp.where((col // CHUNK) <= ((row % t) // CHUNK), s, -jnp.inf)
        m_prev = m_scr[...]
        m_new = jnp.maximum(m_prev, jnp.max(s, axis=1, keepdims=True))
        alpha = jnp.exp(m_prev - m_new)
        p = jnp.exp(s - m_new)
        l_scr[...] = alpha * l_scr[...] + jnp.sum(p, axis=1, keepdims=True)
        acc_scr[...] = alpha * acc_scr[...] + jnp.dot(p.astype(BF16), v,
                                                      preferred_element_type=F32)
        m_scr[...] = m_new

    def body(i, carry):
        step(pl.multiple_of(i * t, t), False)
        return carry

    lax.fori_loop(0, qi, body, 0)
    step(pl.multiple_of(qi * t, t), True)

    lp = lam_ref[...]
    lam = (jnp.exp(jnp.sum(lp[0:1, :] * lp[1:2, :], axis=1, keepdims=True))
           - jnp.exp(jnp.sum(lp[2:3, :] * lp[3:4, :], axis=1, keepdims=True)) + lam_init)
    inv_l = 1.0 / l_scr[...]
    o = acc_scr[0:t, :] * inv_l[0:t, :] - lam * (acc_scr[t:2 * t, :] * inv_l[t:2 * t, :])
    o_ref[...] = (_rms(o, subln_ref[...], SUBLN_EPS) * (1.0 - lam_init)).astype(BF16)


def _diff_attn(lam_p, subln, qq, kk, vv, lam_init):
    s, qk_w = qq.shape
    heads = qk_w // LANES
    t = ATTN_TILE
    return pl.pallas_call(
        functools.partial(_diff_attn_kernel, lam_init=lam_init),
        grid=(heads, s // t),
        in_specs=[
            pl.BlockSpec(lam_p.shape, lambda h, i: (0, 0)),
            pl.BlockSpec(subln.shape, lambda h, i: (0, 0)),
            pl.BlockSpec((t, LANES), lambda h, i: (i, h)),
            pl.BlockSpec((s, LANES), lambda h, i: (0, h)),
            pl.BlockSpec((s, LANES), lambda h, i: (0, h)),
        ],
        out_specs=pl.BlockSpec((t, LANES), lambda h, i: (i, h)),
        out_shape=jax.ShapeDtypeStruct((s, heads * LANES), BF16),
        scratch_shapes=[pltpu.VMEM((2 * t, LANES), BF16),
                        pltpu.VMEM((2 * t, 1), F32),
                        pltpu.VMEM((2 * t, 1), F32),
                        pltpu.VMEM((2 * t, LANES), F32)],
        compiler_params=pltpu.CompilerParams(dimension_semantics=("arbitrary", "arbitrary"),
                                             vmem_limit_bytes=VMEM_LIMIT),
        name="diff_attn",
    )(lam_p, subln, qq, kk, vv)


def _b_out_kernel(x_ref, main_ref, qm_ref, qgain_ref, mk_ref, mv_ref, wo_ref, o_ref):
    main_w = main_ref.shape[1]
    mo = _mem_attention(qm_ref[...], qgain_ref[...], mk_ref, mv_ref, _group_matrix()).astype(BF16)
    o_ref[...] = (x_ref[...]
                  + jnp.dot(main_ref[...], wo_ref[0:main_w, :], preferred_element_type=F32)
                  + jnp.dot(mo, wo_ref[main_w:, :], preferred_element_type=F32))


def _b_out(x, main, qm, qgain, mk, mv, w_o):
    s, d = x.shape
    tm = ROW_TILE
    return pl.pallas_call(
        _b_out_kernel,
        grid=(s // tm,),
        in_specs=[pl.BlockSpec((tm, d), lambda i: (i, 0)),
                  pl.BlockSpec((tm, main.shape[1]), lambda i: (i, 0)),
                  pl.BlockSpec((tm, qm.shape[1]), lambda i: (i, 0)),
                  _resident(qgain.shape), _resident(mk.shape), _resident(mv.shape),
                  _resident(w_o.shape)],
        out_specs=pl.BlockSpec((tm, d), lambda i: (i, 0)),
        out_shape=jax.ShapeDtypeStruct((s, d), F32),
        compiler_params=pltpu.CompilerParams(dimension_semantics=("arbitrary",),
                                             vmem_limit_bytes=VMEM_LIMIT),
        name="b_out",
    )(x, main, qm, qgain, mk, mv, w_o)


def _rope_tables(seq):
    half = ROPE_DIM // 2
    inv = 1.0 / (ROPE_THETA ** (jnp.arange(0, ROPE_DIM, 2, dtype=F32) / ROPE_DIM))
    ang = jnp.arange(seq, dtype=F32)[:, None] * inv[None, :]
    cos, sin = jnp.cos(ang), jnp.sin(ang)
    ones = jnp.ones((seq, HEAD_DIM - ROPE_DIM), F32)
    zeros_half = jnp.zeros((seq, half), F32)
    zeros_rest = jnp.zeros((seq, HEAD_DIM - ROPE_DIM), F32)
    cos_t = jnp.concatenate([cos, cos, ones], axis=1)
    sin_lo = jnp.concatenate([-sin, zeros_half, zeros_rest], axis=1)
    sin_hi = jnp.concatenate([zeros_half, sin, zeros_rest], axis=1)
    reps = LANES // HEAD_DIM
    return jnp.stack([jnp.tile(t, (1, reps)) for t in (cos_t, sin_lo, sin_hi)])


def _pair_heads(w, qk_w):
    d = w.shape[0]
    heads = qk_w // HEAD_DIM
    m1 = w[:, :qk_w].reshape(d, heads, HEAD_DIM)
    m2 = w[:, qk_w:2 * qk_w].reshape(d, heads, HEAD_DIM)
    return jnp.concatenate([jnp.stack([m1, m2], axis=2).reshape(d, 2 * qk_w), w[:, 2 * qk_w:]], axis=1)


def kernel(x, mem, norm_mix, norm_mlp, a_w_in, a_conv, b_w_q, b_q_norm, b_lam, b_subln, kv_norm,
           w_kv, k_norm, mem_norm, w_mem_kv, mem_q_norm, mem_k_norm, w_o, w_up, w_down):
    b, s, d = x.shape
    assert b == 1
    depth = norm_mix.shape[0]
    n_a = a_w_in.shape[0]
    mem_w = w_mem_kv.shape[2] // 2
    qk_w = (b_w_q.shape[2] - mem_w) // 2
    mem_heads = mem_w // HEAD_DIM
    pair = LANES // HEAD_DIM

    xs = x[0]
    rope = _rope_tables(s)
    mk, mv = _mem_prep(mem[0], mem_norm[None, :], w_mem_kv,
                       jnp.tile(mem_k_norm, (1, mem_heads))[:, None, :])
    mem_qgain = jnp.tile(mem_q_norm, (1, mem_heads))
    kk = vv = None
    for l in range(depth):
        g_mix = norm_mix[l][None, :]
        wo_l = w_o[l].astype(BF16)
        if l < n_a:
            xs = _a_mixer(xs, g_mix, a_w_in[l].astype(BF16), a_conv[l], mem_qgain[l][None, :],
                          mk[l], mv[l], wo_l)
        else:
            j = l - n_a
            lam_init = 0.8 - 0.6 * math.exp(-0.3 * l)
            qq, qm = _b_qproj(xs, g_mix, _pair_heads(b_w_q[j], qk_w).astype(BF16),
                              jnp.tile(b_q_norm[j], pair)[None, :], rope, 2 * qk_w)
            main = _diff_attn(b_lam[j], b_subln[j][None, :], qq, kk, vv, lam_init)
            xs = _b_out(xs, main, qm, mem_qgain[l][None, :], mk[l], mv[l], wo_l)
        xs = _mlp(xs, norm_mlp[l][None, :], w_up[l].astype(BF16), w_down[l].astype(BF16))
        if l == n_a - 1:
            kk, vv = _kv_proj(xs, kv_norm[None, :], _pair_heads(w_kv, qk_w).astype(BF16),
                              jnp.tile(k_norm, pair)[None, :], rope, 2 * qk_w)
    return xs[None]
```

```python
import functools
import math

import jax
import jax.numpy as jnp
from jax import lax
from jax.experimental import pallas as pl
from jax.experimental.pallas import tpu as pltpu

F32 = jnp.float32
BF16 = jnp.bfloat16

LANES = 128
HEAD_DIM = 64
CHUNK = 64
ROPE_DIM = HEAD_DIM // 4
ROPE_THETA = 500000.0
CONV_K = 3
EPS = 1e-6
SUBLN_EPS = 1e-5
ROW_TILE = 512
ATTN_TILE = 512
FF_CHUNK = 1024
VMEM_LIMIT = 56 * 1024 * 1024


def _resident(shape):
    nd = len(shape)
    return pl.BlockSpec(shape, lambda *_: (0,) * nd, pipeline_mode=pl.Buffered(1))


def _rms(x, g, eps=EPS):
    return x * lax.rsqrt(jnp.mean(x * x, axis=-1, keepdims=True) + eps) * g


def _group_matrix():
    r = lax.broadcasted_iota(jnp.int32, (LANES, LANES), 0) // HEAD_DIM
    c = lax.broadcasted_iota(jnp.int32, (LANES, LANES), 1) // HEAD_DIM
    return jnp.where(r == c, 1.0 / HEAD_DIM, 0.0).astype(BF16)


def _head_rms(x, gain, gmat):
    ms = jnp.dot((x * x).astype(BF16), gmat, preferred_element_type=F32)
    return x * lax.rsqrt(ms + EPS) * gain


def _rope(x, cos_t, sin_lo, sin_hi):
    return (x * cos_t + pltpu.roll(x, LANES - ROPE_DIM // 2, 1) * sin_lo
            + pltpu.roll(x, ROPE_DIM // 2, 1) * sin_hi)


def _mem_attention(qm, qgain, mk_ref, mv_ref, gmat):
    outs = []
    for g in range(qm.shape[1] // LANES):
        sl = slice(g * LANES, (g + 1) * LANES)
        qn = _head_rms(qm[:, sl], qgain[:, sl], gmat)
        lane = lax.broadcasted_iota(jnp.int32, qn.shape, 1)
        kg = mk_ref[:, sl]
        ps = []
        for hh in range(LANES // HEAD_DIM):
            keep = (lane < HEAD_DIM) if hh == 0 else (lane >= HEAD_DIM)
            qh = jnp.where(keep, qn, 0.0).astype(BF16)
            s = lax.dot_general(qh, kg, (((1,), (1,)), ((), ())), preferred_element_type=F32)
            e = jnp.exp(s - jnp.max(s, axis=-1, keepdims=True))
            p = e * (1.0 / jnp.sum(e, axis=-1, keepdims=True))
            ps.append(p.astype(BF16))
        outs.append(jnp.dot(jnp.concatenate(ps, axis=1), mv_ref[g], preferred_element_type=F32))
    return jnp.concatenate(outs, axis=1)


def _mem_prep_kernel(mem_ref, mnorm_ref, w_ref, kgain_ref, mk_ref, mv_ref):
    gmat = _group_matrix()
    mem_n = _rms(mem_ref[...], mnorm_ref[...]).astype(BF16)
    kv = jnp.dot(mem_n, w_ref[0].astype(BF16), preferred_element_type=F32)
    mem_w = kv.shape[1] // 2
    scale = HEAD_DIM ** -0.5
    for g in range(mem_w // LANES):
        sl = slice(g * LANES, (g + 1) * LANES)
        kn = _head_rms(kv[:, sl], kgain_ref[0][:, sl], gmat) * scale
        mk_ref[0, :, sl] = kn.astype(BF16)
        v = kv[:, mem_w + g * LANES: mem_w + (g + 1) * LANES]
        lane = lax.broadcasted_iota(jnp.int32, v.shape, 1)
        m = v.shape[0]
        mv_ref[0, g, 0:m, :] = jnp.where(lane < HEAD_DIM, v, 0.0).astype(BF16)
        mv_ref[0, g, m:2 * m, :] = jnp.where(lane >= HEAD_DIM, v, 0.0).astype(BF16)


def _mem_prep(mem, mem_norm, w_mem_kv, k_gain):
    depth, d, two_w = w_mem_kv.shape
    m = mem.shape[0]
    mem_w = two_w // 2
    groups = mem_w // LANES
    return pl.pallas_call(
        _mem_prep_kernel,
        grid=(depth,),
        in_specs=[
            pl.BlockSpec((m, d), lambda l: (0, 0)),
            pl.BlockSpec((1, d), lambda l: (0, 0)),
            pl.BlockSpec((1, d, two_w), lambda l: (l, 0, 0)),
            pl.BlockSpec((1, 1, mem_w), lambda l: (l, 0, 0)),
        ],
        out_specs=[
            pl.BlockSpec((1, m, mem_w), lambda l: (l, 0, 0)),
            pl.BlockSpec((1, groups, 2 * m, LANES), lambda l: (l, 0, 0, 0)),
        ],
        out_shape=[
            jax.ShapeDtypeStruct((depth, m, mem_w), BF16),
            jax.ShapeDtypeStruct((depth, groups, 2 * m, LANES), BF16),
        ],
        compiler_params=pltpu.CompilerParams(dimension_semantics=("arbitrary",)),
        name="mem_prep",
    )(mem, mem_norm, w_mem_kv, k_gain)


def _a_mixer_kernel(x_ref, g_ref, win_ref, conv_ref, qgain_ref, mk_ref, mv_ref, wo_ref,
                    o_ref, u_scr, *, conv_w):
    tm = x_ref.shape[0]
    halo = 8

    @pl.when(pl.program_id(0) == 0)
    def _():
        u_scr[0:halo, :] = jnp.zeros((halo, conv_w), F32)

    x = x_ref[...]
    h = _rms(x, g_ref[...]).astype(BF16)
    proj = jnp.dot(h, win_ref[...], preferred_element_type=F32)
    gate_b = proj[:, 0:conv_w]
    u = proj[:, conv_w:2 * conv_w] * proj[:, 2 * conv_w:3 * conv_w]
    qm = proj[:, 3 * conv_w:]

    u_scr[halo:halo + tm, :] = u
    w = conv_ref[...]
    y = (w[0:1, :] * u_scr[halo - 2:halo - 2 + tm, :]
         + w[1:2, :] * u_scr[halo - 1:halo - 1 + tm, :]
         + w[2:3, :] * u)
    u_scr[0:halo, :] = u[tm - halo:tm, :]
    main = (gate_b * y).astype(BF16)

    mo = _mem_attention(qm, qgain_ref[...], mk_ref, mv_ref, _group_matrix()).astype(BF16)
    o_ref[...] = (x
                  + jnp.dot(main, wo_ref[0:conv_w, :], preferred_element_type=F32)
                  + jnp.dot(mo, wo_ref[conv_w:, :], preferred_element_type=F32))


def _a_mixer(x, g, w_in, conv, qgain, mk, mv, w_o):
    s, d = x.shape
    conv_w = conv.shape[1]
    tm = ROW_TILE
    return pl.pallas_call(
        functools.partial(_a_mixer_kernel, conv_w=conv_w),
        grid=(s // tm,),
        in_specs=[
            pl.BlockSpec((tm, d), lambda i: (i, 0)),
            _resident(g.shape), _resident(w_in.shape), _resident(conv.shape),
            _resident(qgain.shape), _resident(mk.shape), _resident(mv.shape),
            _resident(w_o.shape),
        ],
        out_specs=pl.BlockSpec((tm, d), lambda i: (i, 0)),
        out_shape=jax.ShapeDtypeStruct((s, d), F32),
        scratch_shapes=[pltpu.VMEM((tm + 8, conv_w), F32)],
        compiler_params=pltpu.CompilerParams(dimension_semantics=("arbitrary",),
                                             vmem_limit_bytes=VMEM_LIMIT),
        name="a_mixer",
    )(x, g, w_in, conv, qgain, mk, mv, w_o)


def _mlp_kernel(x_ref, g_ref, wup_ref, wdown_ref, o_ref):
    x = x_ref[...]
    h = _rms(x, g_ref[...]).astype(BF16)
    acc = x
    for c in range(wup_ref.shape[1] // FF_CHUNK):
        sl = slice(c * FF_CHUNK, (c + 1) * FF_CHUNK)
        a = jnp.maximum(jnp.dot(h, wup_ref[:, sl], preferred_element_type=F32), 0.0)
        acc = acc + jnp.dot((a * a).astype(BF16), wdown_ref[sl, :], preferred_element_type=F32)
    o_ref[...] = acc


def _mlp(x, g, w_up, w_down):
    s, d = x.shape
    tm = ROW_TILE
    return pl.pallas_call(
        _mlp_kernel,
        grid=(s // tm,),
        in_specs=[pl.BlockSpec((tm, d), lambda i: (i, 0)),
                  _resident(g.shape), _resident(w_up.shape), _resident(w_down.shape)],
        out_specs=pl.BlockSpec((tm, d), lambda i: (i, 0)),
        out_shape=jax.ShapeDtypeStruct((s, d), F32),
        compiler_params=pltpu.CompilerParams(dimension_semantics=("arbitrary",),
                                             vmem_limit_bytes=VMEM_LIMIT),
        name="mlp",
    )(x, g, w_up, w_down)


def _kv_proj_kernel(x_ref, g_ref, w_ref, kgain_ref, rope_ref, kk_ref, vt_ref, *, qk_w):
    gmat = _group_matrix()
    h = _rms(x_ref[...], g_ref[...]).astype(BF16)
    kv = jnp.dot(h, w_ref[...], preferred_element_type=F32)
    cos_t, sin_lo, sin_hi = rope_ref[0], rope_ref[1], rope_ref[2]
    for g in range(qk_w // LANES):
        sl = slice(g * LANES, (g + 1) * LANES)
        kn = _head_rms(kv[:, sl], kgain_ref[...], gmat)
        kk_ref[:, sl] = _rope(kn, cos_t, sin_lo, sin_hi).astype(BF16)
    for g in range((kv.shape[1] - qk_w) // LANES):
        sl = slice(g * LANES, (g + 1) * LANES)
        vt_ref[0, sl, :] = kv[:, qk_w + g * LANES: qk_w + (g + 1) * LANES].T.astype(BF16)


def _kv_proj(x, g, w_kv, kgain, rope, qk_w):
    s, d = x.shape
    tm = ATTN_TILE
    v_w = w_kv.shape[1] - qk_w
    return pl.pallas_call(
        functools.partial(_kv_proj_kernel, qk_w=qk_w),
        grid=(s // tm,),
        in_specs=[pl.BlockSpec((tm, d), lambda i: (i, 0)),
                  _resident(g.shape), _resident(w_kv.shape), _resident(kgain.shape),
                  pl.BlockSpec((3, tm, LANES), lambda i: (0, i, 0))],
        out_specs=[pl.BlockSpec((tm, qk_w), lambda i: (i, 0)),
                   pl.BlockSpec((1, v_w, tm), lambda i: (i, 0, 0))],
        out_shape=[jax.ShapeDtypeStruct((s, qk_w), BF16),
                   jax.ShapeDtypeStruct((s // tm, v_w, tm), BF16)],
        compiler_params=pltpu.CompilerParams(dimension_semantics=("arbitrary",),
                                             vmem_limit_bytes=VMEM_LIMIT),
        name="kv_proj",
    )(x, g, w_kv, kgain, rope)


def _b_qproj_kernel(x_ref, g_ref, w_ref, qgain_ref, rope_ref, qt_ref, qm_ref, *, qk_w):
    gmat = _group_matrix()
    h = _rms(x_ref[...], g_ref[...]).astype(BF16)
    proj = jnp.dot(h, w_ref[...], preferred_element_type=F32)
    cos_t, sin_lo, sin_hi = rope_ref[0], rope_ref[1], rope_ref[2]
    scale = HEAD_DIM ** -0.5
    for g in range(qk_w // LANES):
        sl = slice(g * LANES, (g + 1) * LANES)
        qn = _head_rms(proj[:, sl], qgain_ref[...], gmat)
        qt_ref[sl, :] = (_rope(qn, cos_t, sin_lo, sin_hi) * scale).T.astype(BF16)
    qm_ref[...] = proj[:, qk_w:]


def _b_qproj(x, g, w_q, qgain, rope, qk_w):
    s, d = x.shape
    tm = ROW_TILE
    mem_w = w_q.shape[1] - qk_w
    return pl.pallas_call(
        functools.partial(_b_qproj_kernel, qk_w=qk_w),
        grid=(s // tm,),
        in_specs=[pl.BlockSpec((tm, d), lambda i: (i, 0)),
                  _resident(g.shape), _resident(w_q.shape), _resident(qgain.shape),
                  pl.BlockSpec((3, tm, LANES), lambda i: (0, i, 0))],
        out_specs=[pl.BlockSpec((qk_w, tm), lambda i: (0, i)),
                   pl.BlockSpec((tm, mem_w), lambda i: (i, 0))],
        out_shape=[jax.ShapeDtypeStruct((qk_w, s), BF16), jax.ShapeDtypeStruct((s, mem_w), F32)],
        compiler_params=pltpu.CompilerParams(dimension_semantics=("arbitrary",),
                                             vmem_limit_bytes=VMEM_LIMIT),
        name="b_qproj",
    )(x, g, w_q, qgain, rope)


def _diff_attn_kernel(lam_ref, subln_ref, qt_ref, kk_ref, vt_ref, o_ref,
                      q2_scr, m_scr, l_scr, acc_scr, *, lam_init):
    t = qt_ref.shape[1]
    qi = pl.program_id(1)
    qt = qt_ref[...]
    row = lax.broadcasted_iota(jnp.int32, qt.shape, 0)
    q2_scr[:, 0:t] = jnp.where(row < HEAD_DIM, qt, jnp.zeros_like(qt))
    q2_scr[:, t:2 * t] = jnp.where(row >= HEAD_DIM, qt, jnp.zeros_like(qt))
    m_scr[...] = jnp.full(m_scr.shape, -jnp.inf, F32)
    l_scr[...] = jnp.zeros(l_scr.shape, F32)
    acc_scr[...] = jnp.zeros(acc_scr.shape, F32)

    def step(j, diagonal):
        k = kk_ref[pl.ds(pl.multiple_of(j * t, t), t), :]
        st = jnp.dot(k, q2_scr[...], preferred_element_type=F32)
        if diagonal:
            key = lax.broadcasted_iota(jnp.int32, st.shape, 0)
            qry = lax.broadcasted_iota(jnp.int32, st.shape, 1)
            st = jnp.where((key // CHUNK) <= ((qry % t) // CHUNK), st, -jnp.inf)
        m_prev = m_scr[...]
        m_new = jnp.maximum(m_prev, jnp.max(st, axis=0, keepdims=True))
        alpha = jnp.exp(m_prev - m_new)
        pt = jnp.exp(st - m_new)
        l_scr[...] = alpha * l_scr[...] + jnp.sum(pt, axis=0, keepdims=True)
        acc_scr[...] = alpha * acc_scr[...] + jnp.dot(vt_ref[j], pt.astype(BF16),
                                                      preferred_element_type=F32)
        m_scr[...] = m_new

    def body(j, carry):
        step(j, False)
        return carry

    lax.fori_loop(0, qi, body, 0)
    step(qi, True)

    lp = lam_ref[...]
    lam = (jnp.exp(jnp.sum(lp[0:1, :] * lp[1:2, :], axis=1, keepdims=True))
           - jnp.exp(jnp.sum(lp[2:3, :] * lp[3:4, :], axis=1, keepdims=True)) + lam_init)
    inv_l = 1.0 / l_scr[...]
    ot = acc_scr[:, 0:t] * inv_l[:, 0:t] - lam * (acc_scr[:, t:2 * t] * inv_l[:, t:2 * t])
    ot = ot * lax.rsqrt(jnp.mean(ot * ot, axis=0, keepdims=True) + SUBLN_EPS)
    o_ref[...] = (ot.T * subln_ref[...] * (1.0 - lam_init)).astype(BF16)


def _diff_attn(lam_p, subln, qt, kk, vt, lam_init):
    qk_w, s = qt.shape
    heads = qk_w // LANES
    t = ATTN_TILE
    return pl.pallas_call(
        functools.partial(_diff_attn_kernel, lam_init=lam_init),
        grid=(heads, s // t),
        in_specs=[
            pl.BlockSpec(lam_p.shape, lambda h, i: (0, 0)),
            pl.BlockSpec(subln.shape, lambda h, i: (0, 0)),
            pl.BlockSpec((LANES, t), lambda h, i: (h, i)),
            pl.BlockSpec((s, LANES), lambda h, i: (0, h)),
            pl.BlockSpec((s // t, LANES, t), lambda h, i: (0, h, 0)),
        ],
        out_specs=pl.BlockSpec((t, LANES), lambda h, i: (i, h)),
        out_shape=jax.ShapeDtypeStruct((s, heads * LANES), BF16),
        scratch_shapes=[pltpu.VMEM((LANES, 2 * t), BF16),
                        pltpu.VMEM((1, 2 * t), F32),
                        pltpu.VMEM((1, 2 * t), F32),
                        pltpu.VMEM((LANES, 2 * t), F32)],
        compiler_params=pltpu.CompilerParams(dimension_semantics=("arbitrary", "arbitrary"),
                                             vmem_limit_bytes=VMEM_LIMIT),
        name="diff_attn",
    )(lam_p, subln, qt, kk, vt)


def _b_out_kernel(x_ref, main_ref, qm_ref, qgain_ref, mk_ref, mv_ref, wo_ref, o_ref):
    main_w = main_ref.shape[1]
    mo = _mem_attention(qm_ref[...], qgain_ref[...], mk_ref, mv_ref, _group_matrix()).astype(BF16)
    o_ref[...] = (x_ref[...]
                  + jnp.dot(main_ref[...], wo_ref[0:main_w, :], preferred_element_type=F32)
                  + jnp.dot(mo, wo_ref[main_w:, :], preferred_element_type=F32))


def _b_out(x, main, qm, qgain, mk, mv, w_o):
    s, d = x.shape
    tm = ROW_TILE
    return pl.pallas_call(
        _b_out_kernel,
        grid=(s // tm,),
        in_specs=[pl.BlockSpec((tm, d), lambda i: (i, 0)),
                  pl.BlockSpec((tm, main.shape[1]), lambda i: (i, 0)),
                  pl.BlockSpec((tm, qm.shape[1]), lambda i: (i, 0)),
                  _resident(qgain.shape), _resident(mk.shape), _resident(mv.shape),
                  _resident(w_o.shape)],
        out_specs=pl.BlockSpec((tm, d), lambda i: (i, 0)),
        out_shape=jax.ShapeDtypeStruct((s, d), F32),
        compiler_params=pltpu.CompilerParams(dimension_semantics=("arbitrary",),
                                             vmem_limit_bytes=VMEM_LIMIT),
        name="b_out",
    )(x, main, qm, qgain, mk, mv, w_o)


def _rope_tables(seq):
    half = ROPE_DIM // 2
    inv = 1.0 / (ROPE_THETA ** (jnp.arange(0, ROPE_DIM, 2, dtype=F32) / ROPE_DIM))
    ang = jnp.arange(seq, dtype=F32)[:, None] * inv[None, :]
    cos, sin = jnp.cos(ang), jnp.sin(ang)
    ones = jnp.ones((seq, HEAD_DIM - ROPE_DIM), F32)
    zeros_half = jnp.zeros((seq, half), F32)
    zeros_rest = jnp.zeros((seq, HEAD_DIM - ROPE_DIM), F32)
    cos_t = jnp.concatenate([cos, cos, ones], axis=1)
    sin_lo = jnp.concatenate([-sin, zeros_half, zeros_rest], axis=1)
    sin_hi = jnp.concatenate([zeros_half, sin, zeros_rest], axis=1)
    reps = LANES // HEAD_DIM
    return jnp.stack([jnp.tile(t, (1, reps)) for t in (cos_t, sin_lo, sin_hi)])


def _pair_heads(w, qk_w):
    d = w.shape[0]
    heads = qk_w // HEAD_DIM
    m1 = w[:, :qk_w].reshape(d, heads, HEAD_DIM)
    m2 = w[:, qk_w:2 * qk_w].reshape(d, heads, HEAD_DIM)
    return jnp.concatenate([jnp.stack([m1, m2], axis=2).reshape(d, 2 * qk_w), w[:, 2 * qk_w:]], axis=1)


def kernel(x, mem, norm_mix, norm_mlp, a_w_in, a_conv, b_w_q, b_q_norm, b_lam, b_subln, kv_norm,
           w_kv, k_norm, mem_norm, w_mem_kv, mem_q_norm, mem_k_norm, w_o, w_up, w_down):
    b, s, d = x.shape
    assert b == 1
    depth = norm_mix.shape[0]
    n_a = a_w_in.shape[0]
    mem_w = w_mem_kv.shape[2] // 2
    qk_w = (b_w_q.shape[2] - mem_w) // 2
    mem_heads = mem_w // HEAD_DIM
    pair = LANES // HEAD_DIM

    xs = x[0]
    rope = _rope_tables(s)
    mk, mv = _mem_prep(mem[0], mem_norm[None, :], w_mem_kv,
                       jnp.tile(mem_k_norm, (1, mem_heads))[:, None, :])
    mem_qgain = jnp.tile(mem_q_norm, (1, mem_heads))
    kk = vt = None
    for l in range(depth):
        g_mix = norm_mix[l][None, :]
        wo_l = w_o[l].astype(BF16)
        if l < n_a:
            xs = _a_mixer(xs, g_mix, a_w_in[l].astype(BF16), a_conv[l], mem_qgain[l][None, :],
                          mk[l], mv[l], wo_l)
        else:
            j = l - n_a
            lam_init = 0.8 - 0.6 * math.exp(-0.3 * l)
            qt, qm = _b_qproj(xs, g_mix, _pair_heads(b_w_q[j], qk_w).astype(BF16),
                              jnp.tile(b_q_norm[j], pair)[None, :], rope, 2 * qk_w)
            main = _diff_attn(b_lam[j], b_subln[j][None, :], qt, kk, vt, lam_init)
            xs = _b_out(xs, main, qm, mem_qgain[l][None, :], mk[l], mv[l], wo_l)
        xs = _mlp(xs, norm_mlp[l][None, :], w_up[l].astype(BF16), w_down[l].astype(BF16))
        if l == n_a - 1:
            kk, vt = _kv_proj(xs, kv_norm[None, :], _pair_heads(w_kv, qk_w).astype(BF16),
                              jnp.tile(k_norm, pair)[None, :], rope, 2 * qk_w)
    return xs[None]
```

```python
import functools
import math

import jax
import jax.numpy as jnp
from jax import lax
from jax.experimental import pallas as pl
from jax.experimental.pallas import tpu as pltpu

F32 = jnp.float32
BF16 = jnp.bfloat16

LANES = 128
HEAD_DIM = 64
CHUNK = 64
ROPE_DIM = HEAD_DIM // 4
ROPE_THETA = 500000.0
CONV_K = 3
EPS = 1e-6
SUBLN_EPS = 1e-5
ROW_TILE = 512
ATTN_TILE = 512
ATTN_LANE_CHUNK = 1024
FF_CHUNK = 1024
LOG2E = math.log2(math.e)
VMEM_LIMIT = 56 * 1024 * 1024


def _resident(shape):
    nd = len(shape)
    return pl.BlockSpec(shape, lambda *_: (0,) * nd, pipeline_mode=pl.Buffered(1))


def _rms(x, g, eps=EPS):
    return x * lax.rsqrt(jnp.mean(x * x, axis=-1, keepdims=True) + eps) * g


def _group_matrix():
    r = lax.broadcasted_iota(jnp.int32, (LANES, LANES), 0) // HEAD_DIM
    c = lax.broadcasted_iota(jnp.int32, (LANES, LANES), 1) // HEAD_DIM
    return jnp.where(r == c, 1.0 / HEAD_DIM, 0.0).astype(BF16)


def _head_rms(x, gain, gmat):
    ms = jnp.dot((x * x).astype(BF16), gmat, preferred_element_type=F32)
    return x * lax.rsqrt(ms + EPS) * gain


def _rope(x, cos_t, sin_lo, sin_hi):
    return (x * cos_t + pltpu.roll(x, LANES - ROPE_DIM // 2, 1) * sin_lo
            + pltpu.roll(x, ROPE_DIM // 2, 1) * sin_hi)


def _mem_attention(qm, qgain, mk_ref, mv_ref, gmat):
    outs = []
    for g in range(qm.shape[1] // LANES):
        sl = slice(g * LANES, (g + 1) * LANES)
        qn = _head_rms(qm[:, sl], qgain[:, sl], gmat)
        lane = lax.broadcasted_iota(jnp.int32, qn.shape, 1)
        kg = mk_ref[:, sl]
        ps = []
        for hh in range(LANES // HEAD_DIM):
            keep = (lane < HEAD_DIM) if hh == 0 else (lane >= HEAD_DIM)
            qh = jnp.where(keep, qn, 0.0).astype(BF16)
            s = lax.dot_general(qh, kg, (((1,), (1,)), ((), ())), preferred_element_type=F32)
            e = jnp.exp(s - jnp.max(s, axis=-1, keepdims=True))
            p = e * (1.0 / jnp.sum(e, axis=-1, keepdims=True))
            ps.append(p.astype(BF16))
        outs.append(jnp.dot(jnp.concatenate(ps, axis=1), mv_ref[g], preferred_element_type=F32))
    return jnp.concatenate(outs, axis=1)


def _mem_prep_kernel(mem_ref, mnorm_ref, w_ref, kgain_ref, mk_ref, mv_ref):
    gmat = _group_matrix()
    mem_n = _rms(mem_ref[...], mnorm_ref[...]).astype(BF16)
    kv = jnp.dot(mem_n, w_ref[0].astype(BF16), preferred_element_type=F32)
    mem_w = kv.shape[1] // 2
    scale = HEAD_DIM ** -0.5
    for g in range(mem_w // LANES):
        sl = slice(g * LANES, (g + 1) * LANES)
        kn = _head_rms(kv[:, sl], kgain_ref[0][:, sl], gmat) * scale
        mk_ref[0, :, sl] = kn.astype(BF16)
        v = kv[:, mem_w + g * LANES: mem_w + (g + 1) * LANES]
        lane = lax.broadcasted_iota(jnp.int32, v.shape, 1)
        m = v.shape[0]
        mv_ref[0, g, 0:m, :] = jnp.where(lane < HEAD_DIM, v, 0.0).astype(BF16)
        mv_ref[0, g, m:2 * m, :] = jnp.where(lane >= HEAD_DIM, v, 0.0).astype(BF16)


def _mem_prep(mem, mem_norm, w_mem_kv, k_gain):
    depth, d, two_w = w_mem_kv.shape
    m = mem.shape[0]
    mem_w = two_w // 2
    groups = mem_w // LANES
    return pl.pallas_call(
        _mem_prep_kernel,
        grid=(depth,),
        in_specs=[
            pl.BlockSpec((m, d), lambda l: (0, 0)),
            pl.BlockSpec((1, d), lambda l: (0, 0)),
            pl.BlockSpec((1, d, two_w), lambda l: (l, 0, 0)),
            pl.BlockSpec((1, 1, mem_w), lambda l: (l, 0, 0)),
        ],
        out_specs=[
            pl.BlockSpec((1, m, mem_w), lambda l: (l, 0, 0)),
            pl.BlockSpec((1, groups, 2 * m, LANES), lambda l: (l, 0, 0, 0)),
        ],
        out_shape=[
            jax.ShapeDtypeStruct((depth, m, mem_w), BF16),
            jax.ShapeDtypeStruct((depth, groups, 2 * m, LANES), BF16),
        ],
        compiler_params=pltpu.CompilerParams(dimension_semantics=("arbitrary",)),
        name="mem_prep",
    )(mem, mem_norm, w_mem_kv, k_gain)


def _a_mixer_kernel(x_ref, g_ref, win_ref, conv_ref, qgain_ref, mk_ref, mv_ref, wo_ref,
                    o_ref, u_scr, *, conv_w):
    tm = x_ref.shape[0]
    halo = 8

    @pl.when(pl.program_id(0) == 0)
    def _():
        u_scr[0:halo, :] = jnp.zeros((halo, conv_w), F32)

    x = x_ref[...]
    h = _rms(x, g_ref[...]).astype(BF16)
    proj = jnp.dot(h, win_ref[...], preferred_element_type=F32)
    gate_b = proj[:, 0:conv_w]
    u = proj[:, conv_w:2 * conv_w] * proj[:, 2 * conv_w:3 * conv_w]
    qm = proj[:, 3 * conv_w:]

    u_scr[halo:halo + tm, :] = u
    w = conv_ref[...]
    y = (w[0:1, :] * u_scr[halo - 2:halo - 2 + tm, :]
         + w[1:2, :] * u_scr[halo - 1:halo - 1 + tm, :]
         + w[2:3, :] * u)
    u_scr[0:halo, :] = u[tm - halo:tm, :]
    main = (gate_b * y).astype(BF16)

    mo = _mem_attention(qm, qgain_ref[...], mk_ref, mv_ref, _group_matrix()).astype(BF16)
    o_ref[...] = (x
                  + jnp.dot(main, wo_ref[0:conv_w, :], preferred_element_type=F32)
                  + jnp.dot(mo, wo_ref[conv_w:, :], preferred_element_type=F32))


def _a_mixer(x, g, w_in, conv, qgain, mk, mv, w_o):
    s, d = x.shape
    conv_w = conv.shape[1]
    tm = ROW_TILE
    return pl.pallas_call(
        functools.partial(_a_mixer_kernel, conv_w=conv_w),
        grid=(s // tm,),
        in_specs=[
            pl.BlockSpec((tm, d), lambda i: (i, 0)),
            _resident(g.shape), _resident(w_in.shape), _resident(conv.shape),
            _resident(qgain.shape), _resident(mk.shape), _resident(mv.shape),
            _resident(w_o.shape),
        ],
        out_specs=pl.BlockSpec((tm, d), lambda i: (i, 0)),
        out_shape=jax.ShapeDtypeStruct((s, d), F32),
        scratch_shapes=[pltpu.VMEM((tm + 8, conv_w), F32)],
        compiler_params=pltpu.CompilerParams(dimension_semantics=("arbitrary",),
                                             vmem_limit_bytes=VMEM_LIMIT),
        name="a_mixer",
    )(x, g, w_in, conv, qgain, mk, mv, w_o)


def _mlp_kernel(x_ref, g_ref, wup_ref, wdown_ref, o_ref):
    x = x_ref[...]
    h = _rms(x, g_ref[...]).astype(BF16)
    acc = x
    for c in range(wup_ref.shape[1] // FF_CHUNK):
        sl = slice(c * FF_CHUNK, (c + 1) * FF_CHUNK)
        a = jnp.maximum(jnp.dot(h, wup_ref[:, sl], preferred_element_type=F32), 0.0)
        acc = acc + jnp.dot((a * a).astype(BF16), wdown_ref[sl, :], preferred_element_type=F32)
    o_ref[...] = acc


def _mlp(x, g, w_up, w_down):
    s, d = x.shape
    tm = ROW_TILE
    return pl.pallas_call(
        _mlp_kernel,
        grid=(s // tm,),
        in_specs=[pl.BlockSpec((tm, d), lambda i: (i, 0)),
                  _resident(g.shape), _resident(w_up.shape), _resident(w_down.shape)],
        out_specs=pl.BlockSpec((tm, d), lambda i: (i, 0)),
        out_shape=jax.ShapeDtypeStruct((s, d), F32),
        compiler_params=pltpu.CompilerParams(dimension_semantics=("arbitrary",),
                                             vmem_limit_bytes=VMEM_LIMIT),
        name="mlp",
    )(x, g, w_up, w_down)


def _kv_proj_kernel(x_ref, g_ref, w_ref, kgain_ref, rope_ref, kk_ref, vt_ref, *, qk_w):
    gmat = _group_matrix()
    h = _rms(x_ref[...], g_ref[...]).astype(BF16)
    kv = jnp.dot(h, w_ref[...], preferred_element_type=F32)
    cos_t, sin_lo, sin_hi = rope_ref[0], rope_ref[1], rope_ref[2]
    for g in range(qk_w // LANES):
        sl = slice(g * LANES, (g + 1) * LANES)
        kn = _head_rms(kv[:, sl], kgain_ref[...], gmat)
        kk_ref[:, sl] = _rope(kn, cos_t, sin_lo, sin_hi).astype(BF16)
    for g in range((kv.shape[1] - qk_w) // LANES):
        sl = slice(g * LANES, (g + 1) * LANES)
        vt_ref[0, sl, :] = kv[:, qk_w + g * LANES: qk_w + (g + 1) * LANES].T.astype(BF16)


def _kv_proj(x, g, w_kv, kgain, rope, qk_w):
    s, d = x.shape
    tm = ATTN_TILE
    v_w = w_kv.shape[1] - qk_w
    return pl.pallas_call(
        functools.partial(_kv_proj_kernel, qk_w=qk_w),
        grid=(s // tm,),
        in_specs=[pl.BlockSpec((tm, d), lambda i: (i, 0)),
                  _resident(g.shape), _resident(w_kv.shape), _resident(kgain.shape),
                  pl.BlockSpec((3, tm, LANES), lambda i: (0, i, 0))],
        out_specs=[pl.BlockSpec((tm, qk_w), lambda i: (i, 0)),
                   pl.BlockSpec((1, v_w, tm), lambda i: (i, 0, 0))],
        out_shape=[jax.ShapeDtypeStruct((s, qk_w), BF16),
                   jax.ShapeDtypeStruct((s // tm, v_w, tm), BF16)],
        compiler_params=pltpu.CompilerParams(dimension_semantics=("arbitrary",),
                                             vmem_limit_bytes=VMEM_LIMIT),
        name="kv_proj",
    )(x, g, w_kv, kgain, rope)


def _b_qproj_kernel(x_ref, g_ref, w_ref, qgain_ref, rope_ref, qt_ref, qm_ref, *, qk_w):
    gmat = _group_matrix()
    h = _rms(x_ref[...], g_ref[...]).astype(BF16)
    proj = jnp.dot(h, w_ref[...], preferred_element_type=F32)
    cos_t, sin_lo, sin_hi = rope_ref[0], rope_ref[1], rope_ref[2]
    scale = HEAD_DIM ** -0.5 * LOG2E
    for g in range(qk_w // LANES):
        sl = slice(g * LANES, (g + 1) * LANES)
        qn = _head_rms(proj[:, sl], qgain_ref[...], gmat)
        qt_ref[sl, :] = (_rope(qn, cos_t, sin_lo, sin_hi) * scale).T.astype(BF16)
    qm_ref[...] = proj[:, qk_w:]


def _b_qproj(x, g, w_q, qgain, rope, qk_w):
    s, d = x.shape
    tm = ROW_TILE
    mem_w = w_q.shape[1] - qk_w
    return pl.pallas_call(
        functools.partial(_b_qproj_kernel, qk_w=qk_w),
        grid=(s // tm,),
        in_specs=[pl.BlockSpec((tm, d), lambda i: (i, 0)),
                  _resident(g.shape), _resident(w_q.shape), _resident(qgain.shape),
                  pl.BlockSpec((3, tm, LANES), lambda i: (0, i, 0))],
        out_specs=[pl.BlockSpec((qk_w, tm), lambda i: (0, i)),
                   pl.BlockSpec((tm, mem_w), lambda i: (i, 0))],
        out_shape=[jax.ShapeDtypeStruct((qk_w, s), BF16), jax.ShapeDtypeStruct((s, mem_w), F32)],
        compiler_params=pltpu.CompilerParams(dimension_semantics=("arbitrary",),
                                             vmem_limit_bytes=VMEM_LIMIT),
        name="b_qproj",
    )(x, g, w_q, qgain, rope)


def _diff_attn_kernel(lam_ref, subln_ref, qt_ref, kk_ref, vt_ref, o_ref,
                      q2_scr, m_scr, l_scr, acc_scr, s0_scr, mx0_scr, s1_scr, mx1_scr, *, lam_init):
    t = qt_ref.shape[1]
    qi = pl.program_id(1)
    qt = qt_ref[...]
    row = lax.broadcasted_iota(jnp.int32, qt.shape, 0)
    q2_scr[:, 0:t] = jnp.where(row < HEAD_DIM, qt, jnp.zeros_like(qt))
    q2_scr[:, t:2 * t] = jnp.where(row >= HEAD_DIM, qt, jnp.zeros_like(qt))
    m_scr[...] = jnp.full(m_scr.shape, -jnp.inf, F32)
    l_scr[...] = jnp.zeros(l_scr.shape, F32)
    acc_scr[...] = jnp.zeros(acc_scr.shape, F32)

    def produce(j, buf):
        s_scr, mx_scr = buf
        k = kk_ref[pl.ds(pl.multiple_of(j * t, t), t), :]
        st = jnp.dot(k, q2_scr[...], preferred_element_type=F32)
        s_scr[...] = st
        mx_scr[...] = jnp.max(st, axis=0, keepdims=True)

    def consume(j, buf, diagonal):
        s_scr, mx_scr = buf
        st = s_scr[...]
        if diagonal:
            key = lax.broadcasted_iota(jnp.int32, st.shape, 0)
            qry = lax.broadcasted_iota(jnp.int32, st.shape, 1)
            st = jnp.where((key // CHUNK) <= ((qry % t) // CHUNK), st, -jnp.inf)
            mx = jnp.max(st, axis=0, keepdims=True)
        else:
            mx = mx_scr[...]
        m_prev = m_scr[...]
        m_new = jnp.maximum(m_prev, mx)
        alpha = jnp.exp2(m_prev - m_new)
        pt = jnp.exp2(st - m_new)
        l_scr[...] = alpha * l_scr[...] + jnp.sum(pt, axis=0, keepdims=True)
        acc_scr[...] = alpha * acc_scr[...] + jnp.dot(vt_ref[j], pt.astype(BF16),
                                                      preferred_element_type=F32)
        m_scr[...] = m_new

    buf0, buf1 = (s0_scr, mx0_scr), (s1_scr, mx1_scr)
    produce(0, buf0)

    def pair(p, carry):
        a = 2 * p
        produce(a + 1, buf1)
        consume(a, buf0, False)
        produce(a + 2, buf0)
        consume(a + 1, buf1, False)
        return carry

    lax.fori_loop(0, qi // 2, pair, 0)

    @pl.when(qi % 2 == 0)
    def _():
        consume(qi, buf0, True)

    @pl.when(qi % 2 == 1)
    def _():
        produce(qi, buf1)
        consume(qi - 1, buf0, False)
        consume(qi, buf1, True)

    lp = lam_ref[...]
    lam = (jnp.exp(jnp.sum(lp[0:1, :] * lp[1:2, :], axis=1, keepdims=True))
           - jnp.exp(jnp.sum(lp[2:3, :] * lp[3:4, :], axis=1, keepdims=True)) + lam_init)
    inv_l = 1.0 / l_scr[...]
    ot = acc_scr[:, 0:t] * inv_l[:, 0:t] - lam * (acc_scr[:, t:2 * t] * inv_l[:, t:2 * t])
    ot = ot * lax.rsqrt(jnp.mean(ot * ot, axis=0, keepdims=True) + SUBLN_EPS)
    o_ref[...] = (ot.T * subln_ref[...] * (1.0 - lam_init)).astype(BF16)


def _diff_attn(lam_p, subln, qt, kk, vt, lam_init):
    qk_w, s = qt.shape
    heads = qk_w // LANES
    t = ATTN_TILE
    return pl.pallas_call(
        functools.partial(_diff_attn_kernel, lam_init=lam_init),
        grid=(heads, s // t),
        in_specs=[
            pl.BlockSpec(lam_p.shape, lambda h, i: (0, 0)),
            pl.BlockSpec(subln.shape, lambda h, i: (0, 0)),
            pl.BlockSpec((LANES, t), lambda h, i: (h, i)),
            pl.BlockSpec((s, LANES), lambda h, i: (0, h)),
            pl.BlockSpec((s // t, LANES, t), lambda h, i: (0, h, 0)),
        ],
        out_specs=pl.BlockSpec((t, LANES), lambda h, i: (i, h)),
        out_shape=jax.ShapeDtypeStruct((s, heads * LANES), BF16),
        scratch_shapes=[pltpu.VMEM((LANES, 2 * t), BF16),
                        pltpu.VMEM((1, 2 * t), F32),
                        pltpu.VMEM((1, 2 * t), F32),
                        pltpu.VMEM((LANES, 2 * t), F32),
                        pltpu.VMEM((t, 2 * t), F32), pltpu.VMEM((1, 2 * t), F32),
                        pltpu.VMEM((t, 2 * t), F32), pltpu.VMEM((1, 2 * t), F32)],
        compiler_params=pltpu.CompilerParams(dimension_semantics=("arbitrary", "arbitrary"),
                                             vmem_limit_bytes=VMEM_LIMIT),
        name="diff_attn",
    )(lam_p, subln, qt, kk, vt)


def _b_out_kernel(x_ref, main_ref, qm_ref, qgain_ref, mk_ref, mv_ref, wo_ref, o_ref):
    main_w = main_ref.shape[1]
    mo = _mem_attention(qm_ref[...], qgain_ref[...], mk_ref, mv_ref, _group_matrix()).astype(BF16)
    o_ref[...] = (x_ref[...]
                  + jnp.dot(main_ref[...], wo_ref[0:main_w, :], preferred_element_type=F32)
                  + jnp.dot(mo, wo_ref[main_w:, :], preferred_element_type=F32))


def _b_out(x, main, qm, qgain, mk, mv, w_o):
    s, d = x.shape
    tm = ROW_TILE
    return pl.pallas_call(
        _b_out_kernel,
        grid=(s // tm,),
        in_specs=[pl.BlockSpec((tm, d), lambda i: (i, 0)),
                  pl.BlockSpec((tm, main.shape[1]), lambda i: (i, 0)),
                  pl.BlockSpec((tm, qm.shape[1]), lambda i: (i, 0)),
                  _resident(qgain.shape), _resident(mk.shape), _resident(mv.shape),
                  _resident(w_o.shape)],
        out_specs=pl.BlockSpec((tm, d), lambda i: (i, 0)),
        out_shape=jax.ShapeDtypeStruct((s, d), F32),
        compiler_params=pltpu.CompilerParams(dimension_semantics=("arbitrary",),
                                             vmem_limit_bytes=VMEM_LIMIT),
        name="b_out",
    )(x, main, qm, qgain, mk, mv, w_o)


def _rope_tables(seq):
    half = ROPE_DIM // 2
    inv = 1.0 / (ROPE_THETA ** (jnp.arange(0, ROPE_DIM, 2, dtype=F32) / ROPE_DIM))
    ang = jnp.arange(seq, dtype=F32)[:, None] * inv[None, :]
    cos, sin = jnp.cos(ang), jnp.sin(ang)
    ones = jnp.ones((seq, HEAD_DIM - ROPE_DIM), F32)
    zeros_half = jnp.zeros((seq, half), F32)
    zeros_rest = jnp.zeros((seq, HEAD_DIM - ROPE_DIM), F32)
    cos_t = jnp.concatenate([cos, cos, ones], axis=1)
    sin_lo = jnp.concatenate([-sin, zeros_half, zeros_rest], axis=1)
    sin_hi = jnp.concatenate([zeros_half, sin, zeros_rest], axis=1)
    reps = LANES // HEAD_DIM
    return jnp.stack([jnp.tile(t, (1, reps)) for t in (cos_t, sin_lo, sin_hi)])


def _pair_heads(w, qk_w):
    d = w.shape[0]
    heads = qk_w // HEAD_DIM
    m1 = w[:, :qk_w].reshape(d, heads, HEAD_DIM)
    m2 = w[:, qk_w:2 * qk_w].reshape(d, heads, HEAD_DIM)
    return jnp.concatenate([jnp.stack([m1, m2], axis=2).reshape(d, 2 * qk_w), w[:, 2 * qk_w:]], axis=1)


def kernel(x, mem, norm_mix, norm_mlp, a_w_in, a_conv, b_w_q, b_q_norm, b_lam, b_subln, kv_norm,
           w_kv, k_norm, mem_norm, w_mem_kv, mem_q_norm, mem_k_norm, w_o, w_up, w_down):
    b, s, d = x.shape
    assert b == 1
    depth = norm_mix.shape[0]
    n_a = a_w_in.shape[0]
    mem_w = w_mem_kv.shape[2] // 2
    qk_w = (b_w_q.shape[2] - mem_w) // 2
    mem_heads = mem_w // HEAD_DIM
    pair = LANES // HEAD_DIM

    xs = x[0]
    rope = _rope_tables(s)
    mk, mv = _mem_prep(mem[0], mem_norm[None, :], w_mem_kv,
                       jnp.tile(mem_k_norm, (1, mem_heads))[:, None, :])
    mem_qgain = jnp.tile(mem_q_norm, (1, mem_heads))
    kk = vt = None
    for l in range(depth):
        g_mix = norm_mix[l][None, :]
        wo_l = w_o[l].astype(BF16)
        if l < n_a:
            xs = _a_mixer(xs, g_mix, a_w_in[l].astype(BF16), a_conv[l], mem_qgain[l][None, :],
                          mk[l], mv[l], wo_l)
        else:
            j = l - n_a
            lam_init = 0.8 - 0.6 * math.exp(-0.3 * l)
            qt, qm = _b_qproj(xs, g_mix, _pair_heads(b_w_q[j], qk_w).astype(BF16),
                              jnp.tile(b_q_norm[j], pair)[None, :], rope, 2 * qk_w)
            main = _diff_attn(b_lam[j], b_subln[j][None, :], qt, kk, vt, lam_init)
            xs = _b_out(xs, main, qm, mem_qgain[l][None, :], mk[l], mv[l], wo_l)
        xs = _mlp(xs, norm_mlp[l][None, :], w_up[l].astype(BF16), w_down[l].astype(BF16))
        if l == n_a - 1:
            kk, vt = _kv_proj(xs, kv_norm[None, :], _pair_heads(w_kv, qk_w).astype(BF16),
                              jnp.tile(k_norm, pair)[None, :], rope, 2 * qk_w)
    return xs[None]
```

```python
import functools
import math

import jax
import jax.numpy as jnp
from jax import lax
from jax.experimental import pallas as pl
from jax.experimental.pallas import tpu as pltpu

F32 = jnp.float32
BF16 = jnp.bfloat16

LANES = 128
HEAD_DIM = 64
CHUNK = 64
ROPE_DIM = HEAD_DIM // 4
ROPE_THETA = 500000.0
CONV_K = 3
CONV_HALO = 8
EPS = 1e-6
SUBLN_EPS = 1e-5
ATTN_TILE = 512
ROW_TILE = ATTN_TILE
ONES_ROWS = 16
VT_ROWS = LANES + ONES_ROWS
FF_CHUNK = 1024
LOG2E = math.log2(math.e)
VMEM_LIMIT = 56 * 1024 * 1024


def _resident(shape):
    nd = len(shape)
    return pl.BlockSpec(shape, lambda *_: (0,) * nd, pipeline_mode=pl.Buffered(1))


def _rows(width, tm):
    return pl.BlockSpec((tm, width), lambda i: (i, 0))


def _rms(x, g, eps=EPS):
    return x * lax.rsqrt(jnp.mean(x * x, axis=-1, keepdims=True) + eps) * g


def _group_matrix():
    r = lax.broadcasted_iota(jnp.int32, (LANES, LANES), 0) // HEAD_DIM
    c = lax.broadcasted_iota(jnp.int32, (LANES, LANES), 1) // HEAD_DIM
    return jnp.where(r == c, 1.0 / HEAD_DIM, 0.0).astype(BF16)


def _head_rms(x, gain, gmat):
    ms = jnp.dot((x * x).astype(BF16), gmat, preferred_element_type=F32)
    return x * lax.rsqrt(ms + EPS) * gain


def _rope(x, cos_t, sin_lo, sin_hi):
    return (x * cos_t + pltpu.roll(x, LANES - ROPE_DIM // 2, 1) * sin_lo
            + pltpu.roll(x, ROPE_DIM // 2, 1) * sin_hi)


def _mem_attention(qm, qgain, mk_ref, mv_ref, gmat):
    outs = []
    for g in range(qm.shape[1] // LANES):
        sl = slice(g * LANES, (g + 1) * LANES)
        qn = _head_rms(qm[:, sl], qgain[:, sl], gmat)
        lane = lax.broadcasted_iota(jnp.int32, qn.shape, 1)
        kg = mk_ref[:, sl]
        ps = []
        for hh in range(LANES // HEAD_DIM):
            keep = (lane < HEAD_DIM) if hh == 0 else (lane >= HEAD_DIM)
            qh = jnp.where(keep, qn, 0.0).astype(BF16)
            s = lax.dot_general(qh, kg, (((1,), (1,)), ((), ())), preferred_element_type=F32)
            e = jnp.exp(s - jnp.max(s, axis=-1, keepdims=True))
            p = e * (1.0 / jnp.sum(e, axis=-1, keepdims=True))
            ps.append(p.astype(BF16))
        outs.append(jnp.dot(jnp.concatenate(ps, axis=1), mv_ref[g], preferred_element_type=F32))
    return jnp.concatenate(outs, axis=1)


def _mem_prep_kernel(mem_ref, mnorm_ref, w_ref, kgain_ref, mk_ref, mv_ref):
    gmat = _group_matrix()
    mem_n = _rms(mem_ref[...], mnorm_ref[...]).astype(BF16)
    kv = jnp.dot(mem_n, w_ref[0].astype(BF16), preferred_element_type=F32)
    mem_w = kv.shape[1] // 2
    scale = HEAD_DIM ** -0.5
    for g in range(mem_w // LANES):
        sl = slice(g * LANES, (g + 1) * LANES)
        kn = _head_rms(kv[:, sl], kgain_ref[0][:, sl], gmat) * scale
        mk_ref[0, :, sl] = kn.astype(BF16)
        v = kv[:, mem_w + g * LANES: mem_w + (g + 1) * LANES]
        lane = lax.broadcasted_iota(jnp.int32, v.shape, 1)
        m = v.shape[0]
        mv_ref[0, g, 0:m, :] = jnp.where(lane < HEAD_DIM, v, 0.0).astype(BF16)
        mv_ref[0, g, m:2 * m, :] = jnp.where(lane >= HEAD_DIM, v, 0.0).astype(BF16)


def _mem_prep(mem, mem_norm, w_mem_kv, k_gain):
    depth, d, two_w = w_mem_kv.shape
    m = mem.shape[0]
    mem_w = two_w // 2
    groups = mem_w // LANES
    return pl.pallas_call(
        _mem_prep_kernel,
        grid=(depth,),
        in_specs=[
            pl.BlockSpec((m, d), lambda l: (0, 0)),
            pl.BlockSpec((1, d), lambda l: (0, 0)),
            pl.BlockSpec((1, d, two_w), lambda l: (l, 0, 0)),
            pl.BlockSpec((1, 1, mem_w), lambda l: (l, 0, 0)),
        ],
        out_specs=[
            pl.BlockSpec((1, m, mem_w), lambda l: (l, 0, 0)),
            pl.BlockSpec((1, groups, 2 * m, LANES), lambda l: (l, 0, 0, 0)),
        ],
        out_shape=[
            jax.ShapeDtypeStruct((depth, m, mem_w), BF16),
            jax.ShapeDtypeStruct((depth, groups, 2 * m, LANES), BF16),
        ],
        compiler_params=pltpu.CompilerParams(dimension_semantics=("arbitrary",)),
        name="mem_prep",
    )(mem, mem_norm, w_mem_kv, k_gain)


def _row_call(kernel_fn, name, s, tm, operands, in_specs, out_specs, out_shape, scratch=()):
    return pl.pallas_call(
        kernel_fn,
        grid=(s // tm,),
        in_specs=in_specs,
        out_specs=out_specs,
        out_shape=out_shape,
        scratch_shapes=list(scratch),
        compiler_params=pltpu.CompilerParams(dimension_semantics=("arbitrary",),
                                             vmem_limit_bytes=VMEM_LIMIT),
        name=name,
    )(*operands)


def _mix_out(x, main, qm, qgain_ref, mk_ref, mv_ref, wo_ref):
    main_w = main.shape[1]
    mo = _mem_attention(qm, qgain_ref[...], mk_ref, mv_ref, _group_matrix()).astype(BF16)
    return (x + jnp.dot(main, wo_ref[0:main_w, :], preferred_element_type=F32)
            + jnp.dot(mo, wo_ref[main_w:, :], preferred_element_type=F32))


def _a_mixer_kernel(x_ref, g_ref, win_ref, conv_ref, qgain_ref, mk_ref, mv_ref, wo_ref,
                    o_ref, u_scr):
    tm = x_ref.shape[0]
    conv_w = conv_ref.shape[1]
    halo = CONV_HALO

    @pl.when(pl.program_id(0) == 0)
    def _():
        u_scr[0:halo, :] = jnp.zeros((halo, conv_w), F32)

    x = x_ref[...]
    h = _rms(x, g_ref[...]).astype(BF16)
    proj = jnp.dot(h, win_ref[...], preferred_element_type=F32)
    gate_b = proj[:, 0:conv_w]
    u = proj[:, conv_w:2 * conv_w] * proj[:, 2 * conv_w:3 * conv_w]
    u_scr[halo:halo + tm, :] = u
    w = conv_ref[...]
    y = (w[0:1, :] * u_scr[halo - 2:halo - 2 + tm, :]
         + w[1:2, :] * u_scr[halo - 1:halo - 1 + tm, :]
         + w[2:3, :] * u)
    u_scr[0:halo, :] = u[tm - halo:tm, :]
    main = (gate_b * y).astype(BF16)
    o_ref[...] = _mix_out(x, main, proj[:, 3 * conv_w:], qgain_ref, mk_ref, mv_ref, wo_ref)


def _a_mixer(x, g, w_in, conv, qgain, mk, mv, w_o):
    s, d = x.shape
    tm = ROW_TILE
    resident = (g, w_in, conv, qgain, mk, mv, w_o)
    return _row_call(_a_mixer_kernel, "a_mixer", s, tm, (x, *resident),
                     [_rows(d, tm)] + [_resident(a.shape) for a in resident],
                     _rows(d, tm), jax.ShapeDtypeStruct((s, d), F32),
                     scratch=[pltpu.VMEM((tm + CONV_HALO, conv.shape[1]), F32)])


def _b_out_kernel(x_ref, main_ref, qm_ref, qgain_ref, mk_ref, mv_ref, wo_ref, o_ref):
    o_ref[...] = _mix_out(x_ref[...], main_ref[...], qm_ref[...], qgain_ref, mk_ref, mv_ref, wo_ref)


def _b_out(x, main, qm, qgain, mk, mv, w_o):
    s, d = x.shape
    tm = ROW_TILE
    resident = (qgain, mk, mv, w_o)
    return _row_call(_b_out_kernel, "b_out", s, tm, (x, main, qm, *resident),
                     [_rows(d, tm), _rows(main.shape[1], tm), _rows(qm.shape[1], tm)]
                     + [_resident(a.shape) for a in resident],
                     _rows(d, tm), jax.ShapeDtypeStruct((s, d), F32))


def _mlp_kernel(x_ref, g_ref, wup_ref, wdown_ref, o_ref):
    x = x_ref[...]
    h = _rms(x, g_ref[...]).astype(BF16)
    acc = x
    for c in range(wup_ref.shape[1] // FF_CHUNK):
        sl = slice(c * FF_CHUNK, (c + 1) * FF_CHUNK)
        a = jnp.maximum(jnp.dot(h, wup_ref[:, sl], preferred_element_type=F32), 0.0)
        acc = acc + jnp.dot((a * a).astype(BF16), wdown_ref[sl, :], preferred_element_type=F32)
    o_ref[...] = acc


def _mlp(x, g, w_up, w_down):
    s, d = x.shape
    tm = ROW_TILE
    resident = (g, w_up, w_down)
    return _row_call(_mlp_kernel, "mlp", s, tm, (x, *resident),
                     [_rows(d, tm)] + [_resident(a.shape) for a in resident],
                     _rows(d, tm), jax.ShapeDtypeStruct((s, d), F32))


def _kv_proj_kernel(x_ref, rope_ref, g_ref, w_ref, kgain_ref, kk_ref, vt_ref):
    gmat = _group_matrix()
    tm = x_ref.shape[0]
    qk_w = kk_ref.shape[1]
    h = _rms(x_ref[...], g_ref[...]).astype(BF16)
    kv = jnp.dot(h, w_ref[...], preferred_element_type=F32)
    cos_t, sin_lo, sin_hi = rope_ref[0], rope_ref[1], rope_ref[2]
    for g in range(qk_w // LANES):
        sl = slice(g * LANES, (g + 1) * LANES)
        kn = _head_rms(kv[:, sl], kgain_ref[...], gmat)
        kk_ref[:, sl] = _rope(kn, cos_t, sin_lo, sin_hi).astype(BF16)
    for g in range((kv.shape[1] - qk_w) // LANES):
        r0 = g * VT_ROWS
        vt_ref[0, r0:r0 + LANES, :] = kv[:, qk_w + g * LANES: qk_w + (g + 1) * LANES].T.astype(BF16)
        vt_ref[0, r0 + LANES:r0 + VT_ROWS, :] = jnp.ones((ONES_ROWS, tm), BF16)


def _kv_proj(x, rope, g, w_kv, kgain, qk_w):
    s, d = x.shape
    tm = ROW_TILE
    v_rows = (w_kv.shape[1] - qk_w) // LANES * VT_ROWS
    resident = (g, w_kv, kgain)
    return _row_call(_kv_proj_kernel, "kv_proj", s, tm, (x, rope, *resident),
                     [_rows(d, tm), pl.BlockSpec((3, tm, LANES), lambda i: (0, i, 0))]
                     + [_resident(a.shape) for a in resident],
                     [_rows(qk_w, tm), pl.BlockSpec((1, v_rows, tm), lambda i: (i, 0, 0))],
                     [jax.ShapeDtypeStruct((s, qk_w), BF16),
                      jax.ShapeDtypeStruct((s // tm, v_rows, tm), BF16)])


def _b_qproj_kernel(x_ref, rope_ref, g_ref, w_ref, qgain_ref, qt_ref, qm_ref):
    gmat = _group_matrix()
    qk_w = qt_ref.shape[0]
    h = _rms(x_ref[...], g_ref[...]).astype(BF16)
    proj = jnp.dot(h, w_ref[...], preferred_element_type=F32)
    cos_t, sin_lo, sin_hi = rope_ref[0], rope_ref[1], rope_ref[2]
    scale = HEAD_DIM ** -0.5 * LOG2E
    for g in range(qk_w // LANES):
        sl = slice(g * LANES, (g + 1) * LANES)
        qn = _head_rms(proj[:, sl], qgain_ref[...], gmat)
        qt_ref[sl, :] = (_rope(qn, cos_t, sin_lo, sin_hi) * scale).T.astype(BF16)
    qm_ref[...] = proj[:, qk_w:]


def _b_qproj(x, rope, g, w_q, qgain, qk_w):
    s, d = x.shape
    tm = ROW_TILE
    mem_w = w_q.shape[1] - qk_w
    resident = (g, w_q, qgain)
    return _row_call(_b_qproj_kernel, "b_qproj", s, tm, (x, rope, *resident),
                     [_rows(d, tm), pl.BlockSpec((3, tm, LANES), lambda i: (0, i, 0))]
                     + [_resident(a.shape) for a in resident],
                     [pl.BlockSpec((qk_w, tm), lambda i: (0, i)), _rows(mem_w, tm)],
                     [jax.ShapeDtypeStruct((qk_w, s), BF16), jax.ShapeDtypeStruct((s, mem_w), F32)])


def _diff_attn_kernel(lam_ref, subln_ref, qt_ref, kk_ref, vt_ref, o_ref,
                      q2_scr, m_scr, acc_scr, s0_scr, mx0_scr, s1_scr, mx1_scr, *, lam_init):
    t = qt_ref.shape[1]
    qi = pl.program_id(1)
    qt = qt_ref[...]
    row = lax.broadcasted_iota(jnp.int32, qt.shape, 0)
    q2_scr[:, 0:t] = jnp.where(row < HEAD_DIM, qt, jnp.zeros_like(qt))
    q2_scr[:, t:2 * t] = jnp.where(row >= HEAD_DIM, qt, jnp.zeros_like(qt))
    m_scr[...] = jnp.full(m_scr.shape, -jnp.inf, F32)
    acc_scr[...] = jnp.zeros(acc_scr.shape, F32)

    def produce(j, buf):
        s_scr, mx_scr = buf
        k = kk_ref[pl.ds(pl.multiple_of(j * t, t), t), :]
        st = jnp.dot(k, q2_scr[...], preferred_element_type=F32)
        s_scr[...] = st
        mx_scr[...] = jnp.max(st, axis=0, keepdims=True)

    def consume(j, buf, diagonal):
        s_scr, mx_scr = buf
        st = s_scr[...]
        if diagonal:
            key = lax.broadcasted_iota(jnp.int32, st.shape, 0)
            qry = lax.broadcasted_iota(jnp.int32, st.shape, 1)
            st = jnp.where((key // CHUNK) <= ((qry % t) // CHUNK), st, -jnp.inf)
            mx = jnp.max(st, axis=0, keepdims=True)
        else:
            mx = mx_scr[...]
        m_prev = m_scr[...]
        m_new = jnp.maximum(m_prev, mx)
        alpha = jnp.exp2(m_prev - m_new)
        pt = jnp.exp2(st - m_new)
        acc_scr[...] = alpha * acc_scr[...] + jnp.dot(vt_ref[j], pt.astype(BF16),
                                                      preferred_element_type=F32)
        m_scr[...] = m_new

    buf0, buf1 = (s0_scr, mx0_scr), (s1_scr, mx1_scr)
    produce(0, buf0)

    def pair(p, carry):
        a = 2 * p
        produce(a + 1, buf1)
        consume(a, buf0, False)
        produce(a + 2, buf0)
        consume(a + 1, buf1, False)
        return carry

    lax.fori_loop(0, qi // 2, pair, 0)

    @pl.when(qi % 2 == 0)
    def _():
        consume(qi, buf0, True)

    @pl.when(qi % 2 == 1)
    def _():
        produce(qi, buf1)
        consume(qi - 1, buf0, False)
        consume(qi, buf1, True)

    lp = lam_ref[...]
    lam = (jnp.exp(jnp.sum(lp[0:1, :] * lp[1:2, :], axis=1, keepdims=True))
           - jnp.exp(jnp.sum(lp[2:3, :] * lp[3:4, :], axis=1, keepdims=True)) + lam_init)
    inv_l = 1.0 / acc_scr[LANES:LANES + 1, :]
    ot = (acc_scr[0:LANES, 0:t] * inv_l[:, 0:t]
          - lam * (acc_scr[0:LANES, t:2 * t] * inv_l[:, t:2 * t]))
    ot = ot * lax.rsqrt(jnp.mean(ot * ot, axis=0, keepdims=True) + SUBLN_EPS)
    o_ref[...] = (ot.T * subln_ref[...] * (1.0 - lam_init)).astype(BF16)


def _diff_attn(lam_p, subln, qt, kk, vt, lam_init):
    qk_w, s = qt.shape
    heads = qk_w // LANES
    t = ATTN_TILE
    return pl.pallas_call(
        functools.partial(_diff_attn_kernel, lam_init=lam_init),
        grid=(heads, s // t),
        in_specs=[
            pl.BlockSpec(lam_p.shape, lambda h, i: (0, 0)),
            pl.BlockSpec(subln.shape, lambda h, i: (0, 0)),
            pl.BlockSpec((LANES, t), lambda h, i: (h, i)),
            pl.BlockSpec((s, LANES), lambda h, i: (0, h)),
            pl.BlockSpec((s // t, VT_ROWS, t), lambda h, i: (0, h, 0)),
        ],
        out_specs=pl.BlockSpec((t, LANES), lambda h, i: (i, h)),
        out_shape=jax.ShapeDtypeStruct((s, heads * LANES), BF16),
        scratch_shapes=[pltpu.VMEM((LANES, 2 * t), BF16),
                        pltpu.VMEM((1, 2 * t), F32),
                        pltpu.VMEM((VT_ROWS, 2 * t), F32),
                        pltpu.VMEM((t, 2 * t), F32), pltpu.VMEM((1, 2 * t), F32),
                        pltpu.VMEM((t, 2 * t), F32), pltpu.VMEM((1, 2 * t), F32)],
        compiler_params=pltpu.CompilerParams(dimension_semantics=("arbitrary", "arbitrary"),
                                             vmem_limit_bytes=VMEM_LIMIT),
        name="diff_attn",
    )(lam_p, subln, qt, kk, vt)


def _rope_tables(seq):
    half = ROPE_DIM // 2
    inv = 1.0 / (ROPE_THETA ** (jnp.arange(0, ROPE_DIM, 2, dtype=F32) / ROPE_DIM))
    ang = jnp.arange(seq, dtype=F32)[:, None] * inv[None, :]
    cos, sin = jnp.cos(ang), jnp.sin(ang)
    ones = jnp.ones((seq, HEAD_DIM - ROPE_DIM), F32)
    zeros_half = jnp.zeros((seq, half), F32)
    zeros_rest = jnp.zeros((seq, HEAD_DIM - ROPE_DIM), F32)
    cos_t = jnp.concatenate([cos, cos, ones], axis=1)
    sin_lo = jnp.concatenate([-sin, zeros_half, zeros_rest], axis=1)
    sin_hi = jnp.concatenate([zeros_half, sin, zeros_rest], axis=1)
    reps = LANES // HEAD_DIM
    return jnp.stack([jnp.tile(t, (1, reps)) for t in (cos_t, sin_lo, sin_hi)])


def _pair_heads(w, qk_w):
    d = w.shape[0]
    heads = qk_w // HEAD_DIM
    m1 = w[:, :qk_w].reshape(d, heads, HEAD_DIM)
    m2 = w[:, qk_w:2 * qk_w].reshape(d, heads, HEAD_DIM)
    return jnp.concatenate([jnp.stack([m1, m2], axis=2).reshape(d, 2 * qk_w), w[:, 2 * qk_w:]], axis=1)


def kernel(x, mem, norm_mix, norm_mlp, a_w_in, a_conv, b_w_q, b_q_norm, b_lam, b_subln, kv_norm,
           w_kv, k_norm, mem_norm, w_mem_kv, mem_q_norm, mem_k_norm, w_o, w_up, w_down):
    b, s, d = x.shape
    assert b == 1
    depth = norm_mix.shape[0]
    n_a = a_w_in.shape[0]
    mem_w = w_mem_kv.shape[2] // 2
    qk_w = (b_w_q.shape[2] - mem_w) // 2
    mem_heads = mem_w // HEAD_DIM
    pair = LANES // HEAD_DIM

    xs = x[0]
    rope = _rope_tables(s)
    mk, mv = _mem_prep(mem[0], mem_norm[None, :], w_mem_kv,
                       jnp.tile(mem_k_norm, (1, mem_heads))[:, None, :])
    mem_qgain = jnp.tile(mem_q_norm, (1, mem_heads))
    kk = vt = None
    for l in range(depth):
        g_mix = norm_mix[l][None, :]
        wo_l = w_o[l].astype(BF16)
        if l < n_a:
            xs = _a_mixer(xs, g_mix, a_w_in[l].astype(BF16), a_conv[l], mem_qgain[l][None, :],
                          mk[l], mv[l], wo_l)
        else:
            j = l - n_a
            lam_init = 0.8 - 0.6 * math.exp(-0.3 * l)
            qt, qm = _b_qproj(xs, rope, g_mix, _pair_heads(b_w_q[j], qk_w).astype(BF16),
                              jnp.tile(b_q_norm[j], pair)[None, :], 2 * qk_w)
            main = _diff_attn(b_lam[j], b_subln[j][None, :], qt, kk, vt, lam_init)
            xs = _b_out(xs, main, qm, mem_qgain[l][None, :], mk[l], mv[l], wo_l)
        xs = _mlp(xs, norm_mlp[l][None, :], w_up[l].astype(BF16), w_down[l].astype(BF16))
        if l == n_a - 1:
            kk, vt = _kv_proj(xs, rope, kv_norm[None, :], _pair_heads(w_kv, qk_w).astype(BF16),
                              jnp.tile(k_norm, pair)[None, :], 2 * qk_w)
    return xs[None]
```

```python
import functools
import math

import jax
import jax.numpy as jnp
from jax import lax
from jax.experimental import pallas as pl
from jax.experimental.pallas import tpu as pltpu

F32 = jnp.float32
BF16 = jnp.bfloat16

LANES = 128
HEAD_DIM = 64
CHUNK = 64
ROPE_DIM = HEAD_DIM // 4
ROPE_THETA = 500000.0
CONV_K = 3
CONV_HALO = 8
EPS = 1e-6
SUBLN_EPS = 1e-5
ATTN_TILE = 512
ROW_TILE = ATTN_TILE
ONES_ROWS = 16
VT_ROWS = LANES + ONES_ROWS
FF_CHUNK = 1024
LOG2E = math.log2(math.e)
VMEM_LIMIT = 56 * 1024 * 1024


def _resident(shape):
    nd = len(shape)
    return pl.BlockSpec(shape, lambda *_: (0,) * nd, pipeline_mode=pl.Buffered(1))


def _rows(width, tm):
    return pl.BlockSpec((tm, width), lambda i: (i, 0))


def _rms(x, g, eps=EPS):
    return x * lax.rsqrt(jnp.mean(x * x, axis=-1, keepdims=True) + eps) * g


def _group_matrix():
    r = lax.broadcasted_iota(jnp.int32, (LANES, LANES), 0) // HEAD_DIM
    c = lax.broadcasted_iota(jnp.int32, (LANES, LANES), 1) // HEAD_DIM
    return jnp.where(r == c, 1.0 / HEAD_DIM, 0.0).astype(BF16)


def _head_rms(x, gain, gmat):
    ms = jnp.dot((x * x).astype(BF16), gmat, preferred_element_type=F32)
    return x * lax.rsqrt(ms + EPS) * gain


def _rope(x, cos_t, sin_lo, sin_hi):
    return (x * cos_t + pltpu.roll(x, LANES - ROPE_DIM // 2, 1) * sin_lo
            + pltpu.roll(x, ROPE_DIM // 2, 1) * sin_hi)


def _mem_attention(qm, qgain, mk_ref, mv_ref, gmat):
    outs = []
    for g in range(qm.shape[1] // LANES):
        sl = slice(g * LANES, (g + 1) * LANES)
        qn = _head_rms(qm[:, sl], qgain[:, sl], gmat)
        lane = lax.broadcasted_iota(jnp.int32, qn.shape, 1)
        kg = mk_ref[:, sl]
        ps = []
        for hh in range(LANES // HEAD_DIM):
            keep = (lane < HEAD_DIM) if hh == 0 else (lane >= HEAD_DIM)
            qh = jnp.where(keep, qn, 0.0).astype(BF16)
            s = lax.dot_general(qh, kg, (((1,), (1,)), ((), ())), preferred_element_type=F32)
            e = jnp.exp(s - jnp.max(s, axis=-1, keepdims=True))
            p = e * (1.0 / jnp.sum(e, axis=-1, keepdims=True))
            ps.append(p.astype(BF16))
        outs.append(jnp.dot(jnp.concatenate(ps, axis=1), mv_ref[g], preferred_element_type=F32))
    return jnp.concatenate(outs, axis=1)


def _mem_prep_kernel(mem_ref, mnorm_ref, w_ref, kgain_ref, mk_ref, mv_ref):
    gmat = _group_matrix()
    mem_n = _rms(mem_ref[...], mnorm_ref[...]).astype(BF16)
    kv = jnp.dot(mem_n, w_ref[0].astype(BF16), preferred_element_type=F32)
    mem_w = kv.shape[1] // 2
    scale = HEAD_DIM ** -0.5
    for g in range(mem_w // LANES):
        sl = slice(g * LANES, (g + 1) * LANES)
        kn = _head_rms(kv[:, sl], kgain_ref[0][:, sl], gmat) * scale
        mk_ref[0, :, sl] = kn.astype(BF16)
        v = kv[:, mem_w + g * LANES: mem_w + (g + 1) * LANES]
        lane = lax.broadcasted_iota(jnp.int32, v.shape, 1)
        m = v.shape[0]
        mv_ref[0, g, 0:m, :] = jnp.where(lane < HEAD_DIM, v, 0.0).astype(BF16)
        mv_ref[0, g, m:2 * m, :] = jnp.where(lane >= HEAD_DIM, v, 0.0).astype(BF16)


def _mem_prep(mem, mem_norm, w_mem_kv, k_gain):
    depth, d, two_w = w_mem_kv.shape
    m = mem.shape[0]
    mem_w = two_w // 2
    groups = mem_w // LANES
    return pl.pallas_call(
        _mem_prep_kernel,
        grid=(depth,),
        in_specs=[
            pl.BlockSpec((m, d), lambda l: (0, 0)),
            pl.BlockSpec((1, d), lambda l: (0, 0)),
            pl.BlockSpec((1, d, two_w), lambda l: (l, 0, 0)),
            pl.BlockSpec((1, 1, mem_w), lambda l: (l, 0, 0)),
        ],
        out_specs=[
            pl.BlockSpec((1, m, mem_w), lambda l: (l, 0, 0)),
            pl.BlockSpec((1, groups, 2 * m, LANES), lambda l: (l, 0, 0, 0)),
        ],
        out_shape=[
            jax.ShapeDtypeStruct((depth, m, mem_w), BF16),
            jax.ShapeDtypeStruct((depth, groups, 2 * m, LANES), BF16),
        ],
        compiler_params=pltpu.CompilerParams(dimension_semantics=("arbitrary",)),
        name="mem_prep",
    )(mem, mem_norm, w_mem_kv, k_gain)


def _row_call(kernel_fn, name, s, tm, operands, in_specs, out_specs, out_shape, scratch=()):
    return pl.pallas_call(
        kernel_fn,
        grid=(s // tm,),
        in_specs=in_specs,
        out_specs=out_specs,
        out_shape=out_shape,
        scratch_shapes=list(scratch),
        compiler_params=pltpu.CompilerParams(dimension_semantics=("arbitrary",),
                                             vmem_limit_bytes=VMEM_LIMIT),
        name=name,
    )(*operands)


def _mix_out(x, main, qm, qgain_ref, mk_ref, mv_ref, wo_ref):
    main_w = main.shape[1]
    mo = _mem_attention(qm, qgain_ref[...], mk_ref, mv_ref, _group_matrix()).astype(BF16)
    return (x + jnp.dot(main, wo_ref[0:main_w, :], preferred_element_type=F32)
            + jnp.dot(mo, wo_ref[main_w:, :], preferred_element_type=F32))


def _a_mixer_kernel(x_ref, g_ref, win_ref, conv_ref, qgain_ref, mk_ref, mv_ref, wo_ref,
                    o_ref, u_scr):
    tm = x_ref.shape[0]
    conv_w = conv_ref.shape[1]
    halo = CONV_HALO

    @pl.when(pl.program_id(0) == 0)
    def _():
        u_scr[0:halo, :] = jnp.zeros((halo, conv_w), F32)

    x = x_ref[...]
    h = _rms(x, g_ref[...]).astype(BF16)
    proj = jnp.dot(h, win_ref[...], preferred_element_type=F32)
    gate_b = proj[:, 0:conv_w]
    u = proj[:, conv_w:2 * conv_w] * proj[:, 2 * conv_w:3 * conv_w]
    u_scr[halo:halo + tm, :] = u
    w = conv_ref[...]
    y = (w[0:1, :] * u_scr[halo - 2:halo - 2 + tm, :]
         + w[1:2, :] * u_scr[halo - 1:halo - 1 + tm, :]
         + w[2:3, :] * u)
    u_scr[0:halo, :] = u[tm - halo:tm, :]
    main = (gate_b * y).astype(BF16)
    o_ref[...] = _mix_out(x, main, proj[:, 3 * conv_w:], qgain_ref, mk_ref, mv_ref, wo_ref)


def _a_mixer(x, g, w_in, conv, qgain, mk, mv, w_o):
    s, d = x.shape
    tm = ROW_TILE
    resident = (g, w_in, conv, qgain, mk, mv, w_o)
    return _row_call(_a_mixer_kernel, "a_mixer", s, tm, (x, *resident),
                     [_rows(d, tm)] + [_resident(a.shape) for a in resident],
                     _rows(d, tm), jax.ShapeDtypeStruct((s, d), F32),
                     scratch=[pltpu.VMEM((tm + CONV_HALO, conv.shape[1]), F32)])


def _b_out_kernel(x_ref, main_ref, qm_ref, qgain_ref, mk_ref, mv_ref, wo_ref, o_ref):
    o_ref[...] = _mix_out(x_ref[...], main_ref[...], qm_ref[...], qgain_ref, mk_ref, mv_ref, wo_ref)


def _b_out(x, main, qm, qgain, mk, mv, w_o):
    s, d = x.shape
    tm = ROW_TILE
    resident = (qgain, mk, mv, w_o)
    return _row_call(_b_out_kernel, "b_out", s, tm, (x, main, qm, *resident),
                     [_rows(d, tm), _rows(main.shape[1], tm), _rows(qm.shape[1], tm)]
                     + [_resident(a.shape) for a in resident],
                     _rows(d, tm), jax.ShapeDtypeStruct((s, d), F32))


def _mlp_kernel(*refs, emit_kv, emit_q):
    it = iter(refs)
    x_ref, g_ref, wup_ref, wdown_ref = (next(it) for _ in range(4))
    rope_ref = next(it) if (emit_kv or emit_q) else None
    kv_in = [next(it) for _ in range(3)] if emit_kv else None
    q_in = [next(it) for _ in range(3)] if emit_q else None
    o_ref = next(it)
    kv_out = [next(it) for _ in range(2)] if emit_kv else None
    q_out = [next(it) for _ in range(2)] if emit_q else None

    x = x_ref[...]
    h = _rms(x, g_ref[...]).astype(BF16)
    acc = x
    for c in range(wup_ref.shape[1] // FF_CHUNK):
        sl = slice(c * FF_CHUNK, (c + 1) * FF_CHUNK)
        a = jnp.maximum(jnp.dot(h, wup_ref[:, sl], preferred_element_type=F32), 0.0)
        acc = acc + jnp.dot((a * a).astype(BF16), wdown_ref[sl, :], preferred_element_type=F32)
    o_ref[...] = acc
    if emit_kv:
        _emit_kv(acc, rope_ref, *kv_in, *kv_out)
    if emit_q:
        _emit_q(acc, rope_ref, *q_in, *q_out)


def _mlp(x, g, w_up, w_down, rope=None, kv_w=None, q_w=None, qk_w=None):
    s, d = x.shape
    tm = ROW_TILE
    emit_kv, emit_q = kv_w is not None, q_w is not None
    resident = [g, w_up, w_down]
    operands = [x, g, w_up, w_down]
    in_specs = [_rows(d, tm)] + [_resident(a.shape) for a in resident]
    out_specs = [_rows(d, tm)]
    out_shape = [jax.ShapeDtypeStruct((s, d), F32)]
    if emit_kv or emit_q:
        operands.append(rope)
        in_specs.append(pl.BlockSpec((3, tm, LANES), lambda i: (0, i, 0)))
    if emit_kv:
        operands += list(kv_w)
        in_specs += [_resident(a.shape) for a in kv_w]
        v_rows = (kv_w[1].shape[1] - qk_w) // LANES * VT_ROWS
        out_specs += [_rows(qk_w, tm), pl.BlockSpec((1, v_rows, tm), lambda i: (i, 0, 0))]
        out_shape += [jax.ShapeDtypeStruct((s, qk_w), BF16),
                      jax.ShapeDtypeStruct((s // tm, v_rows, tm), BF16)]
    if emit_q:
        operands += list(q_w)
        in_specs += [_resident(a.shape) for a in q_w]
        mem_w = q_w[1].shape[1] - qk_w
        out_specs += [pl.BlockSpec((qk_w, tm), lambda i: (0, i)), _rows(mem_w, tm)]
        out_shape += [jax.ShapeDtypeStruct((qk_w, s), BF16), jax.ShapeDtypeStruct((s, mem_w), F32)]
    return _row_call(functools.partial(_mlp_kernel, emit_kv=emit_kv, emit_q=emit_q), "mlp",
                     s, tm, operands, in_specs, out_specs, out_shape)


def _emit_kv(x, rope_ref, g_ref, w_ref, kgain_ref, kk_ref, vt_ref):
    gmat = _group_matrix()
    tm = x.shape[0]
    qk_w = kk_ref.shape[1]
    h = _rms(x, g_ref[...]).astype(BF16)
    kv = jnp.dot(h, w_ref[...], preferred_element_type=F32)
    cos_t, sin_lo, sin_hi = rope_ref[0], rope_ref[1], rope_ref[2]
    for g in range(qk_w // LANES):
        sl = slice(g * LANES, (g + 1) * LANES)
        kn = _head_rms(kv[:, sl], kgain_ref[...], gmat)
        kk_ref[:, sl] = _rope(kn, cos_t, sin_lo, sin_hi).astype(BF16)
    for g in range((kv.shape[1] - qk_w) // LANES):
        r0 = g * VT_ROWS
        vt_ref[0, r0:r0 + LANES, :] = kv[:, qk_w + g * LANES: qk_w + (g + 1) * LANES].T.astype(BF16)
        vt_ref[0, r0 + LANES:r0 + VT_ROWS, :] = jnp.ones((ONES_ROWS, tm), BF16)


def _emit_q(x, rope_ref, g_ref, w_ref, qgain_ref, qt_ref, qm_ref):
    gmat = _group_matrix()
    qk_w = qt_ref.shape[0]
    h = _rms(x, g_ref[...]).astype(BF16)
    proj = jnp.dot(h, w_ref[...], preferred_element_type=F32)
    cos_t, sin_lo, sin_hi = rope_ref[0], rope_ref[1], rope_ref[2]
    scale = HEAD_DIM ** -0.5 * LOG2E
    for g in range(qk_w // LANES):
        sl = slice(g * LANES, (g + 1) * LANES)
        qn = _head_rms(proj[:, sl], qgain_ref[...], gmat)
        qt_ref[sl, :] = (_rope(qn, cos_t, sin_lo, sin_hi) * scale).T.astype(BF16)
    qm_ref[...] = proj[:, qk_w:]


def _diff_attn_kernel(lam_ref, subln_ref, qt_ref, kk_ref, vt_ref, o_ref,
                      q2_scr, m_scr, acc_scr, s0_scr, mx0_scr, s1_scr, mx1_scr, *, lam_init):
    t = qt_ref.shape[1]
    qi = pl.program_id(1)
    qt = qt_ref[...]
    row = lax.broadcasted_iota(jnp.int32, qt.shape, 0)
    q2_scr[:, 0:t] = jnp.where(row < HEAD_DIM, qt, jnp.zeros_like(qt))
    q2_scr[:, t:2 * t] = jnp.where(row >= HEAD_DIM, qt, jnp.zeros_like(qt))
    m_scr[...] = jnp.full(m_scr.shape, -jnp.inf, F32)
    acc_scr[...] = jnp.zeros(acc_scr.shape, F32)

    def produce(j, buf):
        s_scr, mx_scr = buf
        k = kk_ref[pl.ds(pl.multiple_of(j * t, t), t), :]
        st = jnp.dot(k, q2_scr[...], preferred_element_type=F32)
        s_scr[...] = st
        mx_scr[...] = jnp.max(st, axis=0, keepdims=True)

    def consume(j, buf, diagonal):
        s_scr, mx_scr = buf
        st = s_scr[...]
        if diagonal:
            key = lax.broadcasted_iota(jnp.int32, st.shape, 0)
            qry = lax.broadcasted_iota(jnp.int32, st.shape, 1)
            st = jnp.where((key // CHUNK) <= ((qry % t) // CHUNK), st, -jnp.inf)
            mx = jnp.max(st, axis=0, keepdims=True)
        else:
            mx = mx_scr[...]
        m_prev = m_scr[...]
        m_new = jnp.maximum(m_prev, mx)
        alpha = jnp.exp2(m_prev - m_new)
        pt = jnp.exp2(st - m_new)
        acc_scr[...] = alpha * acc_scr[...] + jnp.dot(vt_ref[j], pt.astype(BF16),
                                                      preferred_element_type=F32)
        m_scr[...] = m_new

    buf0, buf1 = (s0_scr, mx0_scr), (s1_scr, mx1_scr)
    produce(0, buf0)

    def pair(p, carry):
        a = 2 * p
        produce(a + 1, buf1)
        consume(a, buf0, False)
        produce(a + 2, buf0)
        consume(a + 1, buf1, False)
        return carry

    lax.fori_loop(0, qi // 2, pair, 0)

    @pl.when(qi % 2 == 0)
    def _():
        consume(qi, buf0, True)

    @pl.when(qi % 2 == 1)
    def _():
        produce(qi, buf1)
        consume(qi - 1, buf0, False)
        consume(qi, buf1, True)

    lp = lam_ref[...]
    lam = (jnp.exp(jnp.sum(lp[0:1, :] * lp[1:2, :], axis=1, keepdims=True))
           - jnp.exp(jnp.sum(lp[2:3, :] * lp[3:4, :], axis=1, keepdims=True)) + lam_init)
    inv_l = 1.0 / acc_scr[LANES:LANES + 1, :]
    ot = (acc_scr[0:LANES, 0:t] * inv_l[:, 0:t]
          - lam * (acc_scr[0:LANES, t:2 * t] * inv_l[:, t:2 * t]))
    ot = ot * lax.rsqrt(jnp.mean(ot * ot, axis=0, keepdims=True) + SUBLN_EPS)
    o_ref[...] = (ot.T * subln_ref[...] * (1.0 - lam_init)).astype(BF16)


def _diff_attn(lam_p, subln, qt, kk, vt, lam_init):
    qk_w, s = qt.shape
    heads = qk_w // LANES
    t = ATTN_TILE
    return pl.pallas_call(
        functools.partial(_diff_attn_kernel, lam_init=lam_init),
        grid=(heads, s // t),
        in_specs=[
            pl.BlockSpec(lam_p.shape, lambda h, i: (0, 0)),
            pl.BlockSpec(subln.shape, lambda h, i: (0, 0)),
            pl.BlockSpec((LANES, t), lambda h, i: (h, i)),
            pl.BlockSpec((s, LANES), lambda h, i: (0, h)),
            pl.BlockSpec((s // t, VT_ROWS, t), lambda h, i: (0, h, 0)),
        ],
        out_specs=pl.BlockSpec((t, LANES), lambda h, i: (i, h)),
        out_shape=jax.ShapeDtypeStruct((s, heads * LANES), BF16),
        scratch_shapes=[pltpu.VMEM((LANES, 2 * t), BF16),
                        pltpu.VMEM((1, 2 * t), F32),
                        pltpu.VMEM((VT_ROWS, 2 * t), F32),
                        pltpu.VMEM((t, 2 * t), F32), pltpu.VMEM((1, 2 * t), F32),
                        pltpu.VMEM((t, 2 * t), F32), pltpu.VMEM((1, 2 * t), F32)],
        compiler_params=pltpu.CompilerParams(dimension_semantics=("arbitrary", "arbitrary"),
                                             vmem_limit_bytes=VMEM_LIMIT),
        name="diff_attn",
    )(lam_p, subln, qt, kk, vt)


def _rope_tables(seq):
    half = ROPE_DIM // 2
    inv = 1.0 / (ROPE_THETA ** (jnp.arange(0, ROPE_DIM, 2, dtype=F32) / ROPE_DIM))
    ang = jnp.arange(seq, dtype=F32)[:, None] * inv[None, :]
    cos, sin = jnp.cos(ang), jnp.sin(ang)
    ones = jnp.ones((seq, HEAD_DIM - ROPE_DIM), F32)
    zeros_half = jnp.zeros((seq, half), F32)
    zeros_rest = jnp.zeros((seq, HEAD_DIM - ROPE_DIM), F32)
    cos_t = jnp.concatenate([cos, cos, ones], axis=1)
    sin_lo = jnp.concatenate([-sin, zeros_half, zeros_rest], axis=1)
    sin_hi = jnp.concatenate([zeros_half, sin, zeros_rest], axis=1)
    reps = LANES // HEAD_DIM
    return jnp.stack([jnp.tile(t, (1, reps)) for t in (cos_t, sin_lo, sin_hi)])


def _pair_heads(w, qk_w):
    d = w.shape[0]
    heads = qk_w // HEAD_DIM
    m1 = w[:, :qk_w].reshape(d, heads, HEAD_DIM)
    m2 = w[:, qk_w:2 * qk_w].reshape(d, heads, HEAD_DIM)
    return jnp.concatenate([jnp.stack([m1, m2], axis=2).reshape(d, 2 * qk_w), w[:, 2 * qk_w:]], axis=1)


def kernel(x, mem, norm_mix, norm_mlp, a_w_in, a_conv, b_w_q, b_q_norm, b_lam, b_subln, kv_norm,
           w_kv, k_norm, mem_norm, w_mem_kv, mem_q_norm, mem_k_norm, w_o, w_up, w_down):
    b, s, d = x.shape
    assert b == 1
    depth = norm_mix.shape[0]
    n_a = a_w_in.shape[0]
    mem_w = w_mem_kv.shape[2] // 2
    qk_w = (b_w_q.shape[2] - mem_w) // 2
    mem_heads = mem_w // HEAD_DIM
    pair = LANES // HEAD_DIM

    xs = x[0]
    rope = _rope_tables(s)
    mk, mv = _mem_prep(mem[0], mem_norm[None, :], w_mem_kv,
                       jnp.tile(mem_k_norm, (1, mem_heads))[:, None, :])
    mem_qgain = jnp.tile(mem_q_norm, (1, mem_heads))
    kk = vt = qt = qm = None
    for l in range(depth):
        wo_l = w_o[l].astype(BF16)
        if l < n_a:
            xs = _a_mixer(xs, norm_mix[l][None, :], a_w_in[l].astype(BF16), a_conv[l],
                          mem_qgain[l][None, :], mk[l], mv[l], wo_l)
        else:
            j = l - n_a
            lam_init = 0.8 - 0.6 * math.exp(-0.3 * l)
            main = _diff_attn(b_lam[j], b_subln[j][None, :], qt, kk, vt, lam_init)
            xs = _b_out(xs, main, qm, mem_qgain[l][None, :], mk[l], mv[l], wo_l)
        kv_w = q_w = None
        if l == n_a - 1:
            kv_w = (kv_norm[None, :], _pair_heads(w_kv, qk_w).astype(BF16),
                    jnp.tile(k_norm, pair)[None, :])
        if n_a <= l + 1 < depth:
            j = l + 1 - n_a
            q_w = (norm_mix[l + 1][None, :], _pair_heads(b_w_q[j], qk_w).astype(BF16),
                   jnp.tile(b_q_norm[j], pair)[None, :])
        outs = _mlp(xs, norm_mlp[l][None, :], w_up[l].astype(BF16), w_down[l].astype(BF16),
                    rope, kv_w, q_w, 2 * qk_w)
        xs = outs[0]
        if kv_w is not None:
            kk, vt = outs[1], outs[2]
        if q_w is not None:
            qt, qm = outs[-2], outs[-1]
    return xs[None]
```

```python
import functools
import math
from typing import NamedTuple

import jax
import jax.numpy as jnp
from jax import lax
from jax.experimental import pallas as pl
from jax.experimental.pallas import tpu as pltpu

F32 = jnp.float32
BF16 = jnp.bfloat16

LANES = 128
HEAD_DIM = 64
CHUNK = 64
ROPE_DIM = HEAD_DIM // 4
ROPE_THETA = 500000.0
CONV_K = 3
CONV_HALO = 8
EPS = 1e-6
SUBLN_EPS = 1e-5
ATTN_TILE = 512
ROW_TILE = ATTN_TILE
ONES_ROWS = 16
VT_ROWS = LANES + ONES_ROWS
FF_CHUNK = 1024
LOG2E = math.log2(math.e)
VMEM_LIMIT = 56 * 1024 * 1024


def _resident(shape):
    nd = len(shape)
    return pl.BlockSpec(shape, lambda *_: (0,) * nd, pipeline_mode=pl.Buffered(1))


class _LayerOf(NamedTuple):
    stack: jax.Array
    index: int


def _resident_spec(a):
    if isinstance(a, _LayerOf):
        shape, l = a.stack.shape[1:], a.index
        return pl.BlockSpec((None,) + shape, lambda *_: (l,) + (0,) * len(shape),
                            pipeline_mode=pl.Buffered(1))
    return _resident(a.shape)


def _array(a):
    return a.stack if isinstance(a, _LayerOf) else a


def _rows(width, tm):
    return pl.BlockSpec((tm, width), lambda i: (i, 0))


def _rms(x, g, eps=EPS):
    return x * lax.rsqrt(jnp.mean(x * x, axis=-1, keepdims=True) + eps) * g


def _group_matrix():
    r = lax.broadcasted_iota(jnp.int32, (LANES, LANES), 0) // HEAD_DIM
    c = lax.broadcasted_iota(jnp.int32, (LANES, LANES), 1) // HEAD_DIM
    return jnp.where(r == c, 1.0 / HEAD_DIM, 0.0).astype(BF16)


def _head_rms(x, gain, gmat):
    ms = jnp.dot((x * x).astype(BF16), gmat, preferred_element_type=F32)
    return x * lax.rsqrt(ms + EPS) * gain


def _rope(x, cos_t, sin_lo, sin_hi):
    return (x * cos_t + pltpu.roll(x, LANES - ROPE_DIM // 2, 1) * sin_lo
            + pltpu.roll(x, ROPE_DIM // 2, 1) * sin_hi)


def _mem_attention(qm, qgain, mk_ref, mv_ref, gmat):
    outs = []
    for g in range(qm.shape[1] // LANES):
        sl = slice(g * LANES, (g + 1) * LANES)
        qn = _head_rms(qm[:, sl], qgain[:, sl], gmat)
        lane = lax.broadcasted_iota(jnp.int32, qn.shape, 1)
        kg = mk_ref[:, sl]
        ps = []
        for hh in range(LANES // HEAD_DIM):
            keep = (lane < HEAD_DIM) if hh == 0 else (lane >= HEAD_DIM)
            qh = jnp.where(keep, qn, 0.0).astype(BF16)
            s = lax.dot_general(qh, kg, (((1,), (1,)), ((), ())), preferred_element_type=F32)
            e = jnp.exp(s - jnp.max(s, axis=-1, keepdims=True))
            p = e * (1.0 / jnp.sum(e, axis=-1, keepdims=True))
            ps.append(p.astype(BF16))
        outs.append(jnp.dot(jnp.concatenate(ps, axis=1), mv_ref[g], preferred_element_type=F32))
    return jnp.concatenate(outs, axis=1)


def _mem_prep_kernel(mem_ref, mnorm_ref, w_ref, kgain_ref, mk_ref, mv_ref):
    gmat = _group_matrix()
    mem_n = _rms(mem_ref[...], mnorm_ref[...]).astype(BF16)
    kv = jnp.dot(mem_n, w_ref[0].astype(BF16), preferred_element_type=F32)
    mem_w = kv.shape[1] // 2
    scale = HEAD_DIM ** -0.5
    for g in range(mem_w // LANES):
        sl = slice(g * LANES, (g + 1) * LANES)
        kn = _head_rms(kv[:, sl], kgain_ref[0][:, sl], gmat) * scale
        mk_ref[0, :, sl] = kn.astype(BF16)
        v = kv[:, mem_w + g * LANES: mem_w + (g + 1) * LANES]
        lane = lax.broadcasted_iota(jnp.int32, v.shape, 1)
        m = v.shape[0]
        mv_ref[0, g, 0:m, :] = jnp.where(lane < HEAD_DIM, v, 0.0).astype(BF16)
        mv_ref[0, g, m:2 * m, :] = jnp.where(lane >= HEAD_DIM, v, 0.0).astype(BF16)


def _mem_prep(mem, mem_norm, w_mem_kv, k_gain):
    depth, d, two_w = w_mem_kv.shape
    m = mem.shape[0]
    mem_w = two_w // 2
    groups = mem_w // LANES
    return pl.pallas_call(
        _mem_prep_kernel,
        grid=(depth,),
        in_specs=[
            pl.BlockSpec((m, d), lambda l: (0, 0)),
            pl.BlockSpec((1, d), lambda l: (0, 0)),
            pl.BlockSpec((1, d, two_w), lambda l: (l, 0, 0)),
            pl.BlockSpec((1, 1, mem_w), lambda l: (l, 0, 0)),
        ],
        out_specs=[
            pl.BlockSpec((1, m, mem_w), lambda l: (l, 0, 0)),
            pl.BlockSpec((1, groups, 2 * m, LANES), lambda l: (l, 0, 0, 0)),
        ],
        out_shape=[
            jax.ShapeDtypeStruct((depth, m, mem_w), BF16),
            jax.ShapeDtypeStruct((depth, groups, 2 * m, LANES), BF16),
        ],
        compiler_params=pltpu.CompilerParams(dimension_semantics=("arbitrary",)),
        name="mem_prep",
    )(mem, mem_norm, w_mem_kv, k_gain)


def _row_call(kernel_fn, name, s, tm, operands, in_specs, out_specs, out_shape, scratch=()):
    return pl.pallas_call(
        kernel_fn,
        grid=(s // tm,),
        in_specs=in_specs,
        out_specs=out_specs,
        out_shape=out_shape,
        scratch_shapes=list(scratch),
        compiler_params=pltpu.CompilerParams(dimension_semantics=("arbitrary",),
                                             vmem_limit_bytes=VMEM_LIMIT),
        name=name,
    )(*operands)


def _mix_out(x, main, qm, qgain_ref, mk_ref, mv_ref, wo_ref):
    main_w = main.shape[1]
    mo = _mem_attention(qm, qgain_ref[...], mk_ref, mv_ref, _group_matrix()).astype(BF16)
    return (x + jnp.dot(main, wo_ref[0:main_w, :], preferred_element_type=F32)
            + jnp.dot(mo, wo_ref[main_w:, :], preferred_element_type=F32))


def _a_mixer_kernel(x_ref, g_ref, win_ref, conv_ref, qgain_ref, mk_ref, mv_ref, wo_ref,
                    o_ref, u_scr):
    tm = x_ref.shape[0]
    conv_w = conv_ref.shape[1]
    halo = CONV_HALO

    @pl.when(pl.program_id(0) == 0)
    def _():
        u_scr[0:halo, :] = jnp.zeros((halo, conv_w), F32)

    x = x_ref[...]
    h = _rms(x, g_ref[...]).astype(BF16)
    proj = jnp.dot(h, win_ref[...], preferred_element_type=F32)
    gate_b = proj[:, 0:conv_w]
    u = proj[:, conv_w:2 * conv_w] * proj[:, 2 * conv_w:3 * conv_w]
    u_scr[halo:halo + tm, :] = u
    w = conv_ref[...]
    y = (w[0:1, :] * u_scr[halo - 2:halo - 2 + tm, :]
         + w[1:2, :] * u_scr[halo - 1:halo - 1 + tm, :]
         + w[2:3, :] * u)
    u_scr[0:halo, :] = u[tm - halo:tm, :]
    main = (gate_b * y).astype(BF16)
    o_ref[...] = _mix_out(x, main, proj[:, 3 * conv_w:], qgain_ref, mk_ref, mv_ref, wo_ref)


def _a_mixer(x, g, w_in, conv, qgain, mk, mv, w_o):
    s, d = x.shape
    tm = ROW_TILE
    resident = (g, w_in, conv, qgain, mk, mv, w_o)
    return _row_call(_a_mixer_kernel, "a_mixer", s, tm, (x, *map(_array, resident)),
                     [_rows(d, tm)] + [_resident_spec(a) for a in resident],
                     _rows(d, tm), jax.ShapeDtypeStruct((s, d), F32),
                     scratch=[pltpu.VMEM((tm + CONV_HALO, conv.shape[1]), F32)])


def _b_out_kernel(x_ref, main_ref, qm_ref, qgain_ref, mk_ref, mv_ref, wo_ref, o_ref):
    o_ref[...] = _mix_out(x_ref[...], main_ref[...], qm_ref[...], qgain_ref, mk_ref, mv_ref, wo_ref)


def _b_out(x, main, qm, qgain, mk, mv, w_o):
    s, d = x.shape
    tm = ROW_TILE
    resident = (qgain, mk, mv, w_o)
    return _row_call(_b_out_kernel, "b_out", s, tm, (x, main, qm, *map(_array, resident)),
                     [_rows(d, tm), _rows(main.shape[1], tm), _rows(qm.shape[1], tm)]
                     + [_resident_spec(a) for a in resident],
                     _rows(d, tm), jax.ShapeDtypeStruct((s, d), F32))


def _mlp_kernel(*refs, emit_kv, emit_q):
    it = iter(refs)
    x_ref, g_ref, wup_ref, wdown_ref = (next(it) for _ in range(4))
    rope_ref = next(it) if (emit_kv or emit_q) else None
    kv_in = [next(it) for _ in range(3)] if emit_kv else None
    q_in = [next(it) for _ in range(3)] if emit_q else None
    o_ref = next(it)
    kv_out = [next(it) for _ in range(2)] if emit_kv else None
    q_out = [next(it) for _ in range(2)] if emit_q else None

    x = x_ref[...]
    h = _rms(x, g_ref[...]).astype(BF16)
    acc = x
    for c in range(wup_ref.shape[1] // FF_CHUNK):
        sl = slice(c * FF_CHUNK, (c + 1) * FF_CHUNK)
        a = jnp.maximum(jnp.dot(h, wup_ref[:, sl], preferred_element_type=F32), 0.0)
        acc = acc + jnp.dot((a * a).astype(BF16), wdown_ref[sl, :], preferred_element_type=F32)
    o_ref[...] = acc
    if emit_kv:
        _emit_kv(acc, rope_ref, *kv_in, *kv_out)
    if emit_q:
        _emit_q(acc, rope_ref, *q_in, *q_out)


def _mlp(x, g, w_up, w_down, rope=None, kv_w=None, q_w=None, qk_w=None):
    s, d = x.shape
    tm = ROW_TILE
    emit_kv, emit_q = kv_w is not None, q_w is not None
    resident = [g, w_up, w_down]
    operands = [x] + [_array(a) for a in resident]
    in_specs = [_rows(d, tm)] + [_resident_spec(a) for a in resident]
    out_specs = [_rows(d, tm)]
    out_shape = [jax.ShapeDtypeStruct((s, d), F32)]
    if emit_kv or emit_q:
        operands.append(rope)
        in_specs.append(pl.BlockSpec((3, tm, LANES), lambda i: (0, i, 0)))
    if emit_kv:
        operands += list(kv_w)
        in_specs += [_resident(a.shape) for a in kv_w]
        v_rows = (kv_w[1].shape[1] - qk_w) // LANES * VT_ROWS
        out_specs += [_rows(qk_w, tm), pl.BlockSpec((1, v_rows, tm), lambda i: (i, 0, 0))]
        out_shape += [jax.ShapeDtypeStruct((s, qk_w), BF16),
                      jax.ShapeDtypeStruct((s // tm, v_rows, tm), BF16)]
    if emit_q:
        operands += list(q_w)
        in_specs += [_resident(a.shape) for a in q_w]
        mem_w = q_w[1].shape[1] - qk_w
        out_specs += [pl.BlockSpec((qk_w, tm), lambda i: (0, i)), _rows(mem_w, tm)]
        out_shape += [jax.ShapeDtypeStruct((qk_w, s), BF16), jax.ShapeDtypeStruct((s, mem_w), F32)]
    return _row_call(functools.partial(_mlp_kernel, emit_kv=emit_kv, emit_q=emit_q), "mlp",
                     s, tm, operands, in_specs, out_specs, out_shape)


def _emit_kv(x, rope_ref, g_ref, w_ref, kgain_ref, kk_ref, vt_ref):
    gmat = _group_matrix()
    tm = x.shape[0]
    qk_w = kk_ref.shape[1]
    h = _rms(x, g_ref[...]).astype(BF16)
    kv = jnp.dot(h, w_ref[...], preferred_element_type=F32)
    cos_t, sin_lo, sin_hi = rope_ref[0], rope_ref[1], rope_ref[2]
    for g in range(qk_w // LANES):
        sl = slice(g * LANES, (g + 1) * LANES)
        kn = _head_rms(kv[:, sl], kgain_ref[...], gmat)
        kk_ref[:, sl] = _rope(kn, cos_t, sin_lo, sin_hi).astype(BF16)
    for g in range((kv.shape[1] - qk_w) // LANES):
        r0 = g * VT_ROWS
        vt_ref[0, r0:r0 + LANES, :] = kv[:, qk_w + g * LANES: qk_w + (g + 1) * LANES].T.astype(BF16)
        vt_ref[0, r0 + LANES:r0 + VT_ROWS, :] = jnp.ones((ONES_ROWS, tm), BF16)


def _emit_q(x, rope_ref, g_ref, w_ref, qgain_ref, qt_ref, qm_ref):
    gmat = _group_matrix()
    qk_w = qt_ref.shape[0]
    h = _rms(x, g_ref[...]).astype(BF16)
    proj = jnp.dot(h, w_ref[...], preferred_element_type=F32)
    cos_t, sin_lo, sin_hi = rope_ref[0], rope_ref[1], rope_ref[2]
    scale = HEAD_DIM ** -0.5 * LOG2E
    for g in range(qk_w // LANES):
        sl = slice(g * LANES, (g + 1) * LANES)
        qn = _head_rms(proj[:, sl], qgain_ref[...], gmat)
        qt_ref[sl, :] = (_rope(qn, cos_t, sin_lo, sin_hi) * scale).T.astype(BF16)
    qm_ref[...] = proj[:, qk_w:]


def _diff_attn_kernel(lam_ref, subln_ref, qt_ref, kk_ref, vt_ref, o_ref,
                      q2_scr, m_scr, acc_scr, s0_scr, mx0_scr, s1_scr, mx1_scr, *, lam_init):
    t = qt_ref.shape[1]
    qi = pl.program_id(1)
    qt = qt_ref[...]
    row = lax.broadcasted_iota(jnp.int32, qt.shape, 0)
    q2_scr[:, 0:t] = jnp.where(row < HEAD_DIM, qt, jnp.zeros_like(qt))
    q2_scr[:, t:2 * t] = jnp.where(row >= HEAD_DIM, qt, jnp.zeros_like(qt))
    m_scr[...] = jnp.full(m_scr.shape, -jnp.inf, F32)
    acc_scr[...] = jnp.zeros(acc_scr.shape, F32)

    def produce(j, buf):
        s_scr, mx_scr = buf
        k = kk_ref[pl.ds(pl.multiple_of(j * t, t), t), :]
        st = jnp.dot(k, q2_scr[...], preferred_element_type=F32)
        s_scr[...] = st
        mx_scr[...] = jnp.max(st, axis=0, keepdims=True)

    def consume(j, buf, diagonal):
        s_scr, mx_scr = buf
        st = s_scr[...]
        if diagonal:
            key = lax.broadcasted_iota(jnp.int32, st.shape, 0)
            qry = lax.broadcasted_iota(jnp.int32, st.shape, 1)
            st = jnp.where((key // CHUNK) <= ((qry % t) // CHUNK), st, -jnp.inf)
            mx = jnp.max(st, axis=0, keepdims=True)
        else:
            mx = mx_scr[...]
        m_prev = m_scr[...]
        m_new = jnp.maximum(m_prev, mx)
        alpha = jnp.exp2(m_prev - m_new)
        pt = jnp.exp2(st - m_new)
        acc_scr[...] = alpha * acc_scr[...] + jnp.dot(vt_ref[j], pt.astype(BF16),
                                                      preferred_element_type=F32)
        m_scr[...] = m_new

    buf0, buf1 = (s0_scr, mx0_scr), (s1_scr, mx1_scr)
    produce(0, buf0)

    def pair(p, carry):
        a = 2 * p
        produce(a + 1, buf1)
        consume(a, buf0, False)
        produce(a + 2, buf0)
        consume(a + 1, buf1, False)
        return carry

    lax.fori_loop(0, qi // 2, pair, 0)

    @pl.when(qi % 2 == 0)
    def _():
        consume(qi, buf0, True)

    @pl.when(qi % 2 == 1)
    def _():
        produce(qi, buf1)
        consume(qi - 1, buf0, False)
        consume(qi, buf1, True)

    lp = lam_ref[...]
    lam = (jnp.exp(jnp.sum(lp[0:1, :] * lp[1:2, :], axis=1, keepdims=True))
           - jnp.exp(jnp.sum(lp[2:3, :] * lp[3:4, :], axis=1, keepdims=True)) + lam_init)
    inv_l = 1.0 / acc_scr[LANES:LANES + 1, :]
    ot = (acc_scr[0:LANES, 0:t] * inv_l[:, 0:t]
          - lam * (acc_scr[0:LANES, t:2 * t] * inv_l[:, t:2 * t]))
    ot = ot * lax.rsqrt(jnp.mean(ot * ot, axis=0, keepdims=True) + SUBLN_EPS)
    o_ref[...] = (ot.T * subln_ref[...] * (1.0 - lam_init)).astype(BF16)


def _diff_attn(lam_p, subln, qt, kk, vt, lam_init):
    qk_w, s = qt.shape
    heads = qk_w // LANES
    t = ATTN_TILE
    return pl.pallas_call(
        functools.partial(_diff_attn_kernel, lam_init=lam_init),
        grid=(heads, s // t),
        in_specs=[
            pl.BlockSpec(lam_p.shape, lambda h, i: (0, 0)),
            pl.BlockSpec(subln.shape, lambda h, i: (0, 0)),
            pl.BlockSpec((LANES, t), lambda h, i: (h, i)),
            pl.BlockSpec((s, LANES), lambda h, i: (0, h)),
            pl.BlockSpec((s // t, VT_ROWS, t), lambda h, i: (0, h, 0)),
        ],
        out_specs=pl.BlockSpec((t, LANES), lambda h, i: (i, h)),
        out_shape=jax.ShapeDtypeStruct((s, heads * LANES), BF16),
        scratch_shapes=[pltpu.VMEM((LANES, 2 * t), BF16),
                        pltpu.VMEM((1, 2 * t), F32),
                        pltpu.VMEM((VT_ROWS, 2 * t), F32),
                        pltpu.VMEM((t, 2 * t), F32), pltpu.VMEM((1, 2 * t), F32),
                        pltpu.VMEM((t, 2 * t), F32), pltpu.VMEM((1, 2 * t), F32)],
        compiler_params=pltpu.CompilerParams(dimension_semantics=("arbitrary", "arbitrary"),
                                             vmem_limit_bytes=VMEM_LIMIT),
        name="diff_attn",
    )(lam_p, subln, qt, kk, vt)


def _rope_tables(seq):
    half = ROPE_DIM // 2
    inv = 1.0 / (ROPE_THETA ** (jnp.arange(0, ROPE_DIM, 2, dtype=F32) / ROPE_DIM))
    ang = jnp.arange(seq, dtype=F32)[:, None] * inv[None, :]
    cos, sin = jnp.cos(ang), jnp.sin(ang)
    ones = jnp.ones((seq, HEAD_DIM - ROPE_DIM), F32)
    zeros_half = jnp.zeros((seq, half), F32)
    zeros_rest = jnp.zeros((seq, HEAD_DIM - ROPE_DIM), F32)
    cos_t = jnp.concatenate([cos, cos, ones], axis=1)
    sin_lo = jnp.concatenate([-sin, zeros_half, zeros_rest], axis=1)
    sin_hi = jnp.concatenate([zeros_half, sin, zeros_rest], axis=1)
    reps = LANES // HEAD_DIM
    return jnp.stack([jnp.tile(t, (1, reps)) for t in (cos_t, sin_lo, sin_hi)])


def _pair_heads(w, qk_w):
    d = w.shape[0]
    heads = qk_w // HEAD_DIM
    m1 = w[:, :qk_w].reshape(d, heads, HEAD_DIM)
    m2 = w[:, qk_w:2 * qk_w].reshape(d, heads, HEAD_DIM)
    return jnp.concatenate([jnp.stack([m1, m2], axis=2).reshape(d, 2 * qk_w), w[:, 2 * qk_w:]], axis=1)


def kernel(x, mem, norm_mix, norm_mlp, a_w_in, a_conv, b_w_q, b_q_norm, b_lam, b_subln, kv_norm,
           w_kv, k_norm, mem_norm, w_mem_kv, mem_q_norm, mem_k_norm, w_o, w_up, w_down):
    b, s, d = x.shape
    assert b == 1
    depth = norm_mix.shape[0]
    n_a = a_w_in.shape[0]
    mem_w = w_mem_kv.shape[2] // 2
    qk_w = (b_w_q.shape[2] - mem_w) // 2
    mem_heads = mem_w // HEAD_DIM
    pair = LANES // HEAD_DIM

    xs = x[0]
    rope = _rope_tables(s)
    mk, mv = _mem_prep(mem[0], mem_norm[None, :], w_mem_kv,
                       jnp.tile(mem_k_norm, (1, mem_heads))[:, None, :])
    mem_qgain = jnp.tile(mem_q_norm, (1, mem_heads))
    w_in_b, w_o_b, w_up_b, w_down_b = (w.astype(BF16) for w in (a_w_in, w_o, w_up, w_down))
    kk = vt = qt = qm = None
    for l in range(depth):
        wo_l = _LayerOf(w_o_b, l)
        if l < n_a:
            xs = _a_mixer(xs, norm_mix[l][None, :], _LayerOf(w_in_b, l), a_conv[l],
                          mem_qgain[l][None, :], mk[l], mv[l], wo_l)
        else:
            j = l - n_a
            lam_init = 0.8 - 0.6 * math.exp(-0.3 * l)
            main = _diff_attn(b_lam[j], b_subln[j][None, :], qt, kk, vt, lam_init)
            xs = _b_out(xs, main, qm, mem_qgain[l][None, :], mk[l], mv[l], wo_l)
        kv_w = q_w = None
        if l == n_a - 1:
            kv_w = (kv_norm[None, :], _pair_heads(w_kv, qk_w).astype(BF16),
                    jnp.tile(k_norm, pair)[None, :])
        if n_a <= l + 1 < depth:
            j = l + 1 - n_a
            q_w = (norm_mix[l + 1][None, :], _pair_heads(b_w_q[j], qk_w).astype(BF16),
                   jnp.tile(b_q_norm[j], pair)[None, :])
        outs = _mlp(xs, norm_mlp[l][None, :], _LayerOf(w_up_b, l), _LayerOf(w_down_b, l),
                    rope, kv_w, q_w, 2 * qk_w)
        xs = outs[0]
        if kv_w is not None:
            kk, vt = outs[1], outs[2]
        if q_w is not None:
            qt, qm = outs[-2], outs[-1]
    return xs[None]
```

```python
import functools
import math
from typing import NamedTuple

import jax
import jax.numpy as jnp
from jax import lax
from jax.experimental import pallas as pl
from jax.experimental.pallas import tpu as pltpu

F32 = jnp.float32
BF16 = jnp.bfloat16

LANES = 128
HEAD_DIM = 64
CHUNK = 64
ROPE_DIM = HEAD_DIM // 4
ROPE_THETA = 500000.0
CONV_K = 3
CONV_HALO = 8
EPS = 1e-6
SUBLN_EPS = 1e-5
ATTN_TILE = 512
ROW_TILE = ATTN_TILE
ONES_ROWS = 16
VT_ROWS = LANES + ONES_ROWS
FF_CHUNK = 1024
LOG2E = math.log2(math.e)
VMEM_LIMIT = 56 * 1024 * 1024


def _resident(shape):
    nd = len(shape)
    return pl.BlockSpec(shape, lambda *_: (0,) * nd, pipeline_mode=pl.Buffered(1))


class _LayerOf(NamedTuple):
    stack: jax.Array
    index: int


def _resident_spec(a):
    if isinstance(a, _LayerOf):
        shape, l = a.stack.shape[1:], a.index
        return pl.BlockSpec((None,) + shape, lambda *_: (l,) + (0,) * len(shape),
                            pipeline_mode=pl.Buffered(1))
    return _resident(a.shape)


def _array(a):
    return a.stack if isinstance(a, _LayerOf) else a


def _rows(width, tm):
    return pl.BlockSpec((tm, width), lambda i: (i, 0))


def _rms(x, g, eps=EPS):
    return x * lax.rsqrt(jnp.mean(x * x, axis=-1, keepdims=True) + eps) * g


def _group_matrix():
    r = lax.broadcasted_iota(jnp.int32, (LANES, LANES), 0) // HEAD_DIM
    c = lax.broadcasted_iota(jnp.int32, (LANES, LANES), 1) // HEAD_DIM
    return jnp.where(r == c, 1.0 / HEAD_DIM, 0.0).astype(BF16)


def _head_rms(x, gain, gmat):
    ms = jnp.dot((x * x).astype(BF16), gmat, preferred_element_type=F32)
    return x * lax.rsqrt(ms + EPS) * gain


def _rope(x, cos_t, sin_lo, sin_hi):
    return (x * cos_t + pltpu.roll(x, LANES - ROPE_DIM // 2, 1) * sin_lo
            + pltpu.roll(x, ROPE_DIM // 2, 1) * sin_hi)


def _mem_attention(qm, qgain, mk_ref, mv_ref, gmat):
    outs = []
    for g in range(qm.shape[1] // LANES):
        sl = slice(g * LANES, (g + 1) * LANES)
        qn = _head_rms(qm[:, sl], qgain[:, sl], gmat)
        lane = lax.broadcasted_iota(jnp.int32, qn.shape, 1)
        kg = mk_ref[:, sl]
        ps = []
        for hh in range(LANES // HEAD_DIM):
            keep = (lane < HEAD_DIM) if hh == 0 else (lane >= HEAD_DIM)
            qh = jnp.where(keep, qn, 0.0).astype(BF16)
            s = lax.dot_general(qh, kg, (((1,), (1,)), ((), ())), preferred_element_type=F32)
            ps.append(jnp.exp(s - jnp.max(s, axis=-1, keepdims=True)).astype(BF16))
        od = jnp.dot(jnp.concatenate(ps, axis=1), mv_ref[g], preferred_element_type=F32)
        outs.append(od[:, 0:LANES] * (1.0 / od[:, LANES:]))
    return jnp.concatenate(outs, axis=1)


def _mem_prep_kernel(mem_ref, mnorm_ref, w_ref, kgain_ref, mk_ref, mv_ref):
    gmat = _group_matrix()
    mem_n = _rms(mem_ref[...], mnorm_ref[...]).astype(BF16)
    kv = jnp.dot(mem_n, w_ref[0].astype(BF16), preferred_element_type=F32)
    mem_w = kv.shape[1] // 2
    scale = HEAD_DIM ** -0.5
    for g in range(mem_w // LANES):
        sl = slice(g * LANES, (g + 1) * LANES)
        kn = _head_rms(kv[:, sl], kgain_ref[0][:, sl], gmat) * scale
        mk_ref[0, :, sl] = kn.astype(BF16)
        v = kv[:, mem_w + g * LANES: mem_w + (g + 1) * LANES]
        lane = lax.broadcasted_iota(jnp.int32, v.shape, 1)
        m = v.shape[0]
        for hh, keep in enumerate((lane < HEAD_DIM, lane >= HEAD_DIM)):
            rows = slice(hh * m, (hh + 1) * m)
            mv_ref[0, g, rows, 0:LANES] = jnp.where(keep, v, 0.0).astype(BF16)
            mv_ref[0, g, rows, LANES:2 * LANES] = jnp.where(keep, 1.0, 0.0).astype(BF16)


def _mem_prep(mem, mem_norm, w_mem_kv, k_gain):
    depth, d, two_w = w_mem_kv.shape
    m = mem.shape[0]
    mem_w = two_w // 2
    groups = mem_w // LANES
    return pl.pallas_call(
        _mem_prep_kernel,
        grid=(depth,),
        in_specs=[
            pl.BlockSpec((m, d), lambda l: (0, 0)),
            pl.BlockSpec((1, d), lambda l: (0, 0)),
            pl.BlockSpec((1, d, two_w), lambda l: (l, 0, 0)),
            pl.BlockSpec((1, 1, mem_w), lambda l: (l, 0, 0)),
        ],
        out_specs=[
            pl.BlockSpec((1, m, mem_w), lambda l: (l, 0, 0)),
            pl.BlockSpec((1, groups, 2 * m, 2 * LANES), lambda l: (l, 0, 0, 0)),
        ],
        out_shape=[
            jax.ShapeDtypeStruct((depth, m, mem_w), BF16),
            jax.ShapeDtypeStruct((depth, groups, 2 * m, 2 * LANES), BF16),
        ],
        compiler_params=pltpu.CompilerParams(dimension_semantics=("arbitrary",)),
        name="mem_prep",
    )(mem, mem_norm, w_mem_kv, k_gain)


def _row_call(kernel_fn, name, s, tm, operands, in_specs, out_specs, out_shape, scratch=()):
    return pl.pallas_call(
        kernel_fn,
        grid=(s // tm,),
        in_specs=in_specs,
        out_specs=out_specs,
        out_shape=out_shape,
        scratch_shapes=list(scratch),
        compiler_params=pltpu.CompilerParams(dimension_semantics=("arbitrary",),
                                             vmem_limit_bytes=VMEM_LIMIT),
        name=name,
    )(*operands)


def _mix_out(x, main, qm, qgain_ref, mk_ref, mv_ref, wo_ref):
    main_w = main.shape[1]
    mo = _mem_attention(qm, qgain_ref[...], mk_ref, mv_ref, _group_matrix()).astype(BF16)
    return (x + jnp.dot(main, wo_ref[0:main_w, :], preferred_element_type=F32)
            + jnp.dot(mo, wo_ref[main_w:, :], preferred_element_type=F32))


def _a_mixer_kernel(x_ref, g_ref, win_ref, conv_ref, qgain_ref, mk_ref, mv_ref, wo_ref,
                    o_ref, u_scr):
    tm = x_ref.shape[0]
    conv_w = conv_ref.shape[1]
    halo = CONV_HALO

    @pl.when(pl.program_id(0) == 0)
    def _():
        u_scr[0:halo, :] = jnp.zeros((halo, conv_w), F32)

    x = x_ref[...]
    h = _rms(x, g_ref[...]).astype(BF16)
    proj = jnp.dot(h, win_ref[...], preferred_element_type=F32)
    gate_b = proj[:, 0:conv_w]
    u = proj[:, conv_w:2 * conv_w] * proj[:, 2 * conv_w:3 * conv_w]
    u_scr[halo:halo + tm, :] = u
    w = conv_ref[...]
    y = (w[0:1, :] * u_scr[halo - 2:halo - 2 + tm, :]
         + w[1:2, :] * u_scr[halo - 1:halo - 1 + tm, :]
         + w[2:3, :] * u)
    u_scr[0:halo, :] = u[tm - halo:tm, :]
    main = (gate_b * y).astype(BF16)
    o_ref[...] = _mix_out(x, main, proj[:, 3 * conv_w:], qgain_ref, mk_ref, mv_ref, wo_ref)


def _a_mixer(x, g, w_in, conv, qgain, mk, mv, w_o):
    s, d = x.shape
    tm = ROW_TILE
    resident = (g, w_in, conv, qgain, mk, mv, w_o)
    return _row_call(_a_mixer_kernel, "a_mixer", s, tm, (x, *map(_array, resident)),
                     [_rows(d, tm)] + [_resident_spec(a) for a in resident],
                     _rows(d, tm), jax.ShapeDtypeStruct((s, d), F32),
                     scratch=[pltpu.VMEM((tm + CONV_HALO, conv.shape[1]), F32)])


def _b_out_kernel(x_ref, main_ref, qm_ref, qgain_ref, mk_ref, mv_ref, wo_ref, o_ref):
    o_ref[...] = _mix_out(x_ref[...], main_ref[...], qm_ref[...], qgain_ref, mk_ref, mv_ref, wo_ref)


def _b_out(x, main, qm, qgain, mk, mv, w_o):
    s, d = x.shape
    tm = ROW_TILE
    resident = (qgain, mk, mv, w_o)
    return _row_call(_b_out_kernel, "b_out", s, tm, (x, main, qm, *map(_array, resident)),
                     [_rows(d, tm), _rows(main.shape[1], tm), _rows(qm.shape[1], tm)]
                     + [_resident_spec(a) for a in resident],
                     _rows(d, tm), jax.ShapeDtypeStruct((s, d), F32))


def _mlp_kernel(*refs, emit_kv, emit_q):
    it = iter(refs)
    x_ref, g_ref, wup_ref, wdown_ref = (next(it) for _ in range(4))
    rope_ref = next(it) if (emit_kv or emit_q) else None
    kv_in = [next(it) for _ in range(3)] if emit_kv else None
    q_in = [next(it) for _ in range(3)] if emit_q else None
    o_ref = next(it)
    kv_out = [next(it) for _ in range(2)] if emit_kv else None
    q_out = [next(it) for _ in range(2)] if emit_q else None

    x = x_ref[...]
    h = _rms(x, g_ref[...]).astype(BF16)
    acc = x
    for c in range(wup_ref.shape[1] // FF_CHUNK):
        sl = slice(c * FF_CHUNK, (c + 1) * FF_CHUNK)
        a = jnp.maximum(jnp.dot(h, wup_ref[:, sl], preferred_element_type=F32), 0.0)
        acc = acc + jnp.dot((a * a).astype(BF16), wdown_ref[sl, :], preferred_element_type=F32)
    o_ref[...] = acc
    if emit_kv:
        _emit_kv(acc, rope_ref, *kv_in, *kv_out)
    if emit_q:
        _emit_q(acc, rope_ref, *q_in, *q_out)


def _mlp(x, g, w_up, w_down, rope=None, kv_w=None, q_w=None, qk_w=None):
    s, d = x.shape
    tm = ROW_TILE
    emit_kv, emit_q = kv_w is not None, q_w is not None
    resident = [g, w_up, w_down]
    operands = [x] + [_array(a) for a in resident]
    in_specs = [_rows(d, tm)] + [_resident_spec(a) for a in resident]
    out_specs = [_rows(d, tm)]
    out_shape = [jax.ShapeDtypeStruct((s, d), F32)]
    if emit_kv or emit_q:
        operands.append(rope)
        in_specs.append(pl.BlockSpec((3, tm, LANES), lambda i: (0, i, 0)))
    if emit_kv:
        operands += list(kv_w)
        in_specs += [_resident(a.shape) for a in kv_w]
        v_rows = (kv_w[1].shape[1] - qk_w) // LANES * VT_ROWS
        out_specs += [_rows(qk_w, tm), pl.BlockSpec((1, v_rows, tm), lambda i: (i, 0, 0))]
        out_shape += [jax.ShapeDtypeStruct((s, qk_w), BF16),
                      jax.ShapeDtypeStruct((s // tm, v_rows, tm), BF16)]
    if emit_q:
        operands += list(q_w)
        in_specs += [_resident(a.shape) for a in q_w]
        mem_w = q_w[1].shape[1] - qk_w
        out_specs += [pl.BlockSpec((qk_w, tm), lambda i: (0, i)), _rows(mem_w, tm)]
        out_shape += [jax.ShapeDtypeStruct((qk_w, s), BF16), jax.ShapeDtypeStruct((s, mem_w), F32)]
    return _row_call(functools.partial(_mlp_kernel, emit_kv=emit_kv, emit_q=emit_q), "mlp",
                     s, tm, operands, in_specs, out_specs, out_shape)


def _emit_kv(x, rope_ref, g_ref, w_ref, kgain_ref, kk_ref, vt_ref):
    gmat = _group_matrix()
    tm = x.shape[0]
    qk_w = kk_ref.shape[1]
    h = _rms(x, g_ref[...]).astype(BF16)
    kv = jnp.dot(h, w_ref[...], preferred_element_type=F32)
    cos_t, sin_lo, sin_hi = rope_ref[0], rope_ref[1], rope_ref[2]
    for g in range(qk_w // LANES):
        sl = slice(g * LANES, (g + 1) * LANES)
        kn = _head_rms(kv[:, sl], kgain_ref[...], gmat)
        kk_ref[:, sl] = _rope(kn, cos_t, sin_lo, sin_hi).astype(BF16)
    for g in range((kv.shape[1] - qk_w) // LANES):
        r0 = g * VT_ROWS
        vt_ref[0, r0:r0 + LANES, :] = kv[:, qk_w + g * LANES: qk_w + (g + 1) * LANES].T.astype(BF16)
        vt_ref[0, r0 + LANES:r0 + VT_ROWS, :] = jnp.ones((ONES_ROWS, tm), BF16)


def _emit_q(x, rope_ref, g_ref, w_ref, qgain_ref, qt_ref, qm_ref):
    gmat = _group_matrix()
    qk_w = qt_ref.shape[0]
    h = _rms(x, g_ref[...]).astype(BF16)
    proj = jnp.dot(h, w_ref[...], preferred_element_type=F32)
    cos_t, sin_lo, sin_hi = rope_ref[0], rope_ref[1], rope_ref[2]
    scale = HEAD_DIM ** -0.5 * LOG2E
    for g in range(qk_w // LANES):
        sl = slice(g * LANES, (g + 1) * LANES)
        qn = _head_rms(proj[:, sl], qgain_ref[...], gmat)
        qt_ref[sl, :] = (_rope(qn, cos_t, sin_lo, sin_hi) * scale).T.astype(BF16)
    qm_ref[...] = proj[:, qk_w:]


def _diff_attn_kernel(lam_ref, subln_ref, qt_ref, kk_ref, vt_ref, o_ref,
                      q2_scr, m_scr, acc_scr, s0_scr, mx0_scr, s1_scr, mx1_scr, *, lam_init):
    t = qt_ref.shape[1]
    qi = pl.program_id(1)
    qt = qt_ref[...]
    row = lax.broadcasted_iota(jnp.int32, qt.shape, 0)
    q2_scr[:, 0:t] = jnp.where(row < HEAD_DIM, qt, jnp.zeros_like(qt))
    q2_scr[:, t:2 * t] = jnp.where(row >= HEAD_DIM, qt, jnp.zeros_like(qt))
    m_scr[...] = jnp.full(m_scr.shape, -jnp.inf, F32)
    acc_scr[...] = jnp.zeros(acc_scr.shape, F32)

    def produce(j, buf):
        s_scr, mx_scr = buf
        k = kk_ref[pl.ds(pl.multiple_of(j * t, t), t), :]
        st = jnp.dot(k, q2_scr[...], preferred_element_type=F32)
        s_scr[...] = st
        mx_scr[...] = jnp.max(st, axis=0, keepdims=True)

    def consume(j, buf, diagonal):
        s_scr, mx_scr = buf
        st = s_scr[...]
        if diagonal:
            key = lax.broadcasted_iota(jnp.int32, st.shape, 0)
            qry = lax.broadcasted_iota(jnp.int32, st.shape, 1)
            st = jnp.where((key // CHUNK) <= ((qry % t) // CHUNK), st, -jnp.inf)
            mx = jnp.max(st, axis=0, keepdims=True)
        else:
            mx = mx_scr[...]
        m_prev = m_scr[...]
        m_new = jnp.maximum(m_prev, mx)
        alpha = jnp.exp2(m_prev - m_new)
        pt = jnp.exp2(st - m_new)
        acc_scr[...] = alpha * acc_scr[...] + jnp.dot(vt_ref[j], pt.astype(BF16),
                                                      preferred_element_type=F32)
        m_scr[...] = m_new

    buf0, buf1 = (s0_scr, mx0_scr), (s1_scr, mx1_scr)
    produce(0, buf0)

    def pair(p, carry):
        a = 2 * p
        produce(a + 1, buf1)
        consume(a, buf0, False)
        produce(a + 2, buf0)
        consume(a + 1, buf1, False)
        return carry

    lax.fori_loop(0, qi // 2, pair, 0)

    @pl.when(qi % 2 == 0)
    def _():
        consume(qi, buf0, True)

    @pl.when(qi % 2 == 1)
    def _():
        produce(qi, buf1)
        consume(qi - 1, buf0, False)
        consume(qi, buf1, True)

    lp = lam_ref[...]
    lam = (jnp.exp(jnp.sum(lp[0:1, :] * lp[1:2, :], axis=1, keepdims=True))
           - jnp.exp(jnp.sum(lp[2:3, :] * lp[3:4, :], axis=1, keepdims=True)) + lam_init)
    inv_l = 1.0 / acc_scr[LANES:LANES + 1, :]
    ot = (acc_scr[0:LANES, 0:t] * inv_l[:, 0:t]
          - lam * (acc_scr[0:LANES, t:2 * t] * inv_l[:, t:2 * t]))
    ot = ot * lax.rsqrt(jnp.mean(ot * ot, axis=0, keepdims=True) + SUBLN_EPS)
    o_ref[...] = (ot.T * subln_ref[...] * (1.0 - lam_init)).astype(BF16)


def _diff_attn(lam_p, subln, qt, kk, vt, lam_init):
    qk_w, s = qt.shape
    heads = qk_w // LANES
    t = ATTN_TILE
    return pl.pallas_call(
        functools.partial(_diff_attn_kernel, lam_init=lam_init),
        grid=(heads, s // t),
        in_specs=[
            pl.BlockSpec(lam_p.shape, lambda h, i: (0, 0)),
            pl.BlockSpec(subln.shape, lambda h, i: (0, 0)),
            pl.BlockSpec((LANES, t), lambda h, i: (h, i)),
            pl.BlockSpec((s, LANES), lambda h, i: (0, h)),
            pl.BlockSpec((s // t, VT_ROWS, t), lambda h, i: (0, h, 0)),
        ],
        out_specs=pl.BlockSpec((t, LANES), lambda h, i: (i, h)),
        out_shape=jax.ShapeDtypeStruct((s, heads * LANES), BF16),
        scratch_shapes=[pltpu.VMEM((LANES, 2 * t), BF16),
                        pltpu.VMEM((1, 2 * t), F32),
                        pltpu.VMEM((VT_ROWS, 2 * t), F32),
                        pltpu.VMEM((t, 2 * t), F32), pltpu.VMEM((1, 2 * t), F32),
                        pltpu.VMEM((t, 2 * t), F32), pltpu.VMEM((1, 2 * t), F32)],
        compiler_params=pltpu.CompilerParams(dimension_semantics=("arbitrary", "arbitrary"),
                                             vmem_limit_bytes=VMEM_LIMIT),
        name="diff_attn",
    )(lam_p, subln, qt, kk, vt)


def _rope_tables(seq):
    half = ROPE_DIM // 2
    inv = 1.0 / (ROPE_THETA ** (jnp.arange(0, ROPE_DIM, 2, dtype=F32) / ROPE_DIM))
    ang = jnp.arange(seq, dtype=F32)[:, None] * inv[None, :]
    cos, sin = jnp.cos(ang), jnp.sin(ang)
    ones = jnp.ones((seq, HEAD_DIM - ROPE_DIM), F32)
    zeros_half = jnp.zeros((seq, half), F32)
    zeros_rest = jnp.zeros((seq, HEAD_DIM - ROPE_DIM), F32)
    cos_t = jnp.concatenate([cos, cos, ones], axis=1)
    sin_lo = jnp.concatenate([-sin, zeros_half, zeros_rest], axis=1)
    sin_hi = jnp.concatenate([zeros_half, sin, zeros_rest], axis=1)
    reps = LANES // HEAD_DIM
    return jnp.stack([jnp.tile(t, (1, reps)) for t in (cos_t, sin_lo, sin_hi)])


def _pair_heads(w, qk_w):
    d = w.shape[0]
    heads = qk_w // HEAD_DIM
    m1 = w[:, :qk_w].reshape(d, heads, HEAD_DIM)
    m2 = w[:, qk_w:2 * qk_w].reshape(d, heads, HEAD_DIM)
    return jnp.concatenate([jnp.stack([m1, m2], axis=2).reshape(d, 2 * qk_w), w[:, 2 * qk_w:]], axis=1)


def kernel(x, mem, norm_mix, norm_mlp, a_w_in, a_conv, b_w_q, b_q_norm, b_lam, b_subln, kv_norm,
           w_kv, k_norm, mem_norm, w_mem_kv, mem_q_norm, mem_k_norm, w_o, w_up, w_down):
    b, s, d = x.shape
    assert b == 1
    depth = norm_mix.shape[0]
    n_a = a_w_in.shape[0]
    mem_w = w_mem_kv.shape[2] // 2
    qk_w = (b_w_q.shape[2] - mem_w) // 2
    mem_heads = mem_w // HEAD_DIM
    pair = LANES // HEAD_DIM

    xs = x[0]
    rope = _rope_tables(s)
    mk, mv = _mem_prep(mem[0], mem_norm[None, :], w_mem_kv,
                       jnp.tile(mem_k_norm, (1, mem_heads))[:, None, :])
    mem_qgain = jnp.tile(mem_q_norm, (1, mem_heads))
    w_in_b, w_o_b, w_up_b, w_down_b = (w.astype(BF16) for w in (a_w_in, w_o, w_up, w_down))
    kk = vt = qt = qm = None
    for l in range(depth):
        wo_l = _LayerOf(w_o_b, l)
        if l < n_a:
            xs = _a_mixer(xs, norm_mix[l][None, :], _LayerOf(w_in_b, l), a_conv[l],
                          mem_qgain[l][None, :], mk[l], mv[l], wo_l)
        else:
            j = l - n_a
            lam_init = 0.8 - 0.6 * math.exp(-0.3 * l)
            main = _diff_attn(b_lam[j], b_subln[j][None, :], qt, kk, vt, lam_init)
            xs = _b_out(xs, main, qm, mem_qgain[l][None, :], mk[l], mv[l], wo_l)
        kv_w = q_w = None
        if l == n_a - 1:
            kv_w = (kv_norm[None, :], _pair_heads(w_kv, qk_w).astype(BF16),
                    jnp.tile(k_norm, pair)[None, :])
        if n_a <= l + 1 < depth:
            j = l + 1 - n_a
            q_w = (norm_mix[l + 1][None, :], _pair_heads(b_w_q[j], qk_w).astype(BF16),
                   jnp.tile(b_q_norm[j], pair)[None, :])
        outs = _mlp(xs, norm_mlp[l][None, :], _LayerOf(w_up_b, l), _LayerOf(w_down_b, l),
                    rope, kv_w, q_w, 2 * qk_w)
        xs = outs[0]
        if kv_w is not None:
            kk, vt = outs[1], outs[2]
        if q_w is not None:
            qt, qm = outs[-2], outs[-1]
    return xs[None]
```

```python
import functools
import math
from typing import NamedTuple

import jax
import jax.numpy as jnp
from jax import lax
from jax.experimental import pallas as pl
from jax.experimental.pallas import tpu as pltpu

F32 = jnp.float32
BF16 = jnp.bfloat16

LANES = 128
HEAD_DIM = 64
CHUNK = 64
ROPE_DIM = HEAD_DIM // 4
ROPE_THETA = 500000.0
CONV_K = 3
CONV_HALO = 8
EPS = 1e-6
SUBLN_EPS = 1e-5
ATTN_TILE = 1024
ROW_TILE = 512
ONES_ROWS = 16
VT_ROWS = LANES + ONES_ROWS
FF_CHUNK = 1024
LOG2E = math.log2(math.e)
VMEM_LIMIT = 56 * 1024 * 1024


def _resident(shape):
    nd = len(shape)
    return pl.BlockSpec(shape, lambda *_: (0,) * nd, pipeline_mode=pl.Buffered(1))


class _LayerOf(NamedTuple):
    stack: jax.Array
    index: int


def _resident_spec(a):
    if isinstance(a, _LayerOf):
        shape, l = a.stack.shape[1:], a.index
        return pl.BlockSpec((None,) + shape, lambda *_: (l,) + (0,) * len(shape),
                            pipeline_mode=pl.Buffered(1))
    return _resident(a.shape)


def _array(a):
    return a.stack if isinstance(a, _LayerOf) else a


def _rows(width, tm):
    return pl.BlockSpec((tm, width), lambda i: (i, 0))


def _rms(x, g, eps=EPS):
    return x * lax.rsqrt(jnp.mean(x * x, axis=-1, keepdims=True) + eps) * g


def _group_matrix():
    r = lax.broadcasted_iota(jnp.int32, (LANES, LANES), 0) // HEAD_DIM
    c = lax.broadcasted_iota(jnp.int32, (LANES, LANES), 1) // HEAD_DIM
    return jnp.where(r == c, 1.0 / HEAD_DIM, 0.0).astype(BF16)


def _head_rms(x, gain, gmat):
    ms = jnp.dot((x * x).astype(BF16), gmat, preferred_element_type=F32)
    return x * lax.rsqrt(ms + EPS) * gain


def _rope(x, cos_t, sin_lo, sin_hi):
    return (x * cos_t + pltpu.roll(x, LANES - ROPE_DIM // 2, 1) * sin_lo
            + pltpu.roll(x, ROPE_DIM // 2, 1) * sin_hi)


def _mem_attention(qm, qgain, mk_ref, mv_ref, gmat):
    outs = []
    for g in range(qm.shape[1] // LANES):
        sl = slice(g * LANES, (g + 1) * LANES)
        qn = _head_rms(qm[:, sl], qgain[:, sl], gmat)
        lane = lax.broadcasted_iota(jnp.int32, qn.shape, 1)
        kg = mk_ref[:, sl]
        ps = []
        for hh in range(LANES // HEAD_DIM):
            keep = (lane < HEAD_DIM) if hh == 0 else (lane >= HEAD_DIM)
            qh = jnp.where(keep, qn, 0.0).astype(BF16)
            s = lax.dot_general(qh, kg, (((1,), (1,)), ((), ())), preferred_element_type=F32)
            ps.append(jnp.exp(s - jnp.max(s, axis=-1, keepdims=True)).astype(BF16))
        od = jnp.dot(jnp.concatenate(ps, axis=1), mv_ref[g], preferred_element_type=F32)
        outs.append(od[:, 0:LANES] * (1.0 / od[:, LANES:]))
    return jnp.concatenate(outs, axis=1)


def _mem_prep_kernel(mem_ref, mnorm_ref, w_ref, kgain_ref, mk_ref, mv_ref):
    gmat = _group_matrix()
    mem_n = _rms(mem_ref[...], mnorm_ref[...]).astype(BF16)
    kv = jnp.dot(mem_n, w_ref[0].astype(BF16), preferred_element_type=F32)
    mem_w = kv.shape[1] // 2
    scale = HEAD_DIM ** -0.5
    for g in range(mem_w // LANES):
        sl = slice(g * LANES, (g + 1) * LANES)
        kn = _head_rms(kv[:, sl], kgain_ref[0][:, sl], gmat) * scale
        mk_ref[0, :, sl] = kn.astype(BF16)
        v = kv[:, mem_w + g * LANES: mem_w + (g + 1) * LANES]
        lane = lax.broadcasted_iota(jnp.int32, v.shape, 1)
        m = v.shape[0]
        for hh, keep in enumerate((lane < HEAD_DIM, lane >= HEAD_DIM)):
            rows = slice(hh * m, (hh + 1) * m)
            mv_ref[0, g, rows, 0:LANES] = jnp.where(keep, v, 0.0).astype(BF16)
            mv_ref[0, g, rows, LANES:2 * LANES] = jnp.where(keep, 1.0, 0.0).astype(BF16)


def _mem_prep(mem, mem_norm, w_mem_kv, k_gain):
    depth, d, two_w = w_mem_kv.shape
    m = mem.shape[0]
    mem_w = two_w // 2
    groups = mem_w // LANES
    return pl.pallas_call(
        _mem_prep_kernel,
        grid=(depth,),
        in_specs=[
            pl.BlockSpec((m, d), lambda l: (0, 0)),
            pl.BlockSpec((1, d), lambda l: (0, 0)),
            pl.BlockSpec((1, d, two_w), lambda l: (l, 0, 0)),
            pl.BlockSpec((1, 1, mem_w), lambda l: (l, 0, 0)),
        ],
        out_specs=[
            pl.BlockSpec((1, m, mem_w), lambda l: (l, 0, 0)),
            pl.BlockSpec((1, groups, 2 * m, 2 * LANES), lambda l: (l, 0, 0, 0)),
        ],
        out_shape=[
            jax.ShapeDtypeStruct((depth, m, mem_w), BF16),
            jax.ShapeDtypeStruct((depth, groups, 2 * m, 2 * LANES), BF16),
        ],
        compiler_params=pltpu.CompilerParams(dimension_semantics=("arbitrary",)),
        name="mem_prep",
    )(mem, mem_norm, w_mem_kv, k_gain)


def _row_call(kernel_fn, name, s, tm, operands, in_specs, out_specs, out_shape, scratch=()):
    return pl.pallas_call(
        kernel_fn,
        grid=(s // tm,),
        in_specs=in_specs,
        out_specs=out_specs,
        out_shape=out_shape,
        scratch_shapes=list(scratch),
        compiler_params=pltpu.CompilerParams(dimension_semantics=("arbitrary",),
                                             vmem_limit_bytes=VMEM_LIMIT),
        name=name,
    )(*operands)


def _mix_out(x, main, qm, qgain_ref, mk_ref, mv_ref, wo_ref):
    main_w = main.shape[1]
    mo = _mem_attention(qm, qgain_ref[...], mk_ref, mv_ref, _group_matrix()).astype(BF16)
    return (x + jnp.dot(main, wo_ref[0:main_w, :], preferred_element_type=F32)
            + jnp.dot(mo, wo_ref[main_w:, :], preferred_element_type=F32))


def _a_mixer_kernel(x_ref, g_ref, win_ref, conv_ref, qgain_ref, mk_ref, mv_ref, wo_ref,
                    o_ref, u_scr):
    tm = x_ref.shape[0]
    conv_w = conv_ref.shape[1]
    halo = CONV_HALO

    @pl.when(pl.program_id(0) == 0)
    def _():
        u_scr[0:halo, :] = jnp.zeros((halo, conv_w), F32)

    x = x_ref[...]
    h = _rms(x, g_ref[...]).astype(BF16)
    proj = jnp.dot(h, win_ref[...], preferred_element_type=F32)
    gate_b = proj[:, 0:conv_w]
    u = proj[:, conv_w:2 * conv_w] * proj[:, 2 * conv_w:3 * conv_w]
    u_scr[halo:halo + tm, :] = u
    w = conv_ref[...]
    y = (w[0:1, :] * u_scr[halo - 2:halo - 2 + tm, :]
         + w[1:2, :] * u_scr[halo - 1:halo - 1 + tm, :]
         + w[2:3, :] * u)
    u_scr[0:halo, :] = u[tm - halo:tm, :]
    main = (gate_b * y).astype(BF16)
    o_ref[...] = _mix_out(x, main, proj[:, 3 * conv_w:], qgain_ref, mk_ref, mv_ref, wo_ref)


def _a_mixer(x, g, w_in, conv, qgain, mk, mv, w_o):
    s, d = x.shape
    tm = ROW_TILE
    resident = (g, w_in, conv, qgain, mk, mv, w_o)
    return _row_call(_a_mixer_kernel, "a_mixer", s, tm, (x, *map(_array, resident)),
                     [_rows(d, tm)] + [_resident_spec(a) for a in resident],
                     _rows(d, tm), jax.ShapeDtypeStruct((s, d), F32),
                     scratch=[pltpu.VMEM((tm + CONV_HALO, conv.shape[1]), F32)])


def _b_out_kernel(x_ref, main_ref, qm_ref, qgain_ref, mk_ref, mv_ref, wo_ref, o_ref):
    o_ref[...] = _mix_out(x_ref[...], main_ref[...], qm_ref[...], qgain_ref, mk_ref, mv_ref, wo_ref)


def _b_out(x, main, qm, qgain, mk, mv, w_o):
    s, d = x.shape
    tm = ROW_TILE
    resident = (qgain, mk, mv, w_o)
    return _row_call(_b_out_kernel, "b_out", s, tm, (x, main, qm, *map(_array, resident)),
                     [_rows(d, tm), _rows(main.shape[1], tm), _rows(qm.shape[1], tm)]
                     + [_resident_spec(a) for a in resident],
                     _rows(d, tm), jax.ShapeDtypeStruct((s, d), F32))


def _mlp_kernel(*refs, emit_kv, emit_q):
    it = iter(refs)
    x_ref, g_ref, wup_ref, wdown_ref = (next(it) for _ in range(4))
    rope_ref = next(it) if (emit_kv or emit_q) else None
    kv_in = [next(it) for _ in range(3)] if emit_kv else None
    q_in = [next(it) for _ in range(3)] if emit_q else None
    o_ref = next(it)
    kv_out = [next(it) for _ in range(2)] if emit_kv else None
    q_out = [next(it) for _ in range(2)] if emit_q else None

    x = x_ref[...]
    h = _rms(x, g_ref[...]).astype(BF16)
    acc = x
    for c in range(wup_ref.shape[1] // FF_CHUNK):
        sl = slice(c * FF_CHUNK, (c + 1) * FF_CHUNK)
        a = jnp.maximum(jnp.dot(h, wup_ref[:, sl], preferred_element_type=F32), 0.0)
        acc = acc + jnp.dot((a * a).astype(BF16), wdown_ref[sl, :], preferred_element_type=F32)
    o_ref[...] = acc
    if emit_kv:
        _emit_kv(acc, rope_ref, *kv_in, *kv_out)
    if emit_q:
        _emit_q(acc, rope_ref, *q_in, *q_out)


def _mlp(x, g, w_up, w_down, rope=None, kv_w=None, q_w=None, qk_w=None):
    s, d = x.shape
    tm = ROW_TILE
    emit_kv, emit_q = kv_w is not None, q_w is not None
    resident = [g, w_up, w_down]
    operands = [x] + [_array(a) for a in resident]
    in_specs = [_rows(d, tm)] + [_resident_spec(a) for a in resident]
    out_specs = [_rows(d, tm)]
    out_shape = [jax.ShapeDtypeStruct((s, d), F32)]
    if emit_kv or emit_q:
        operands.append(rope)
        in_specs.append(pl.BlockSpec((3, tm, LANES), lambda i: (0, i, 0)))
    if emit_kv:
        operands += list(kv_w)
        in_specs += [_resident(a.shape) for a in kv_w]
        v_rows = (kv_w[1].shape[1] - qk_w) // LANES * VT_ROWS
        per_key_tile = ATTN_TILE // tm
        out_specs += [_rows(qk_w, tm),
                      pl.BlockSpec((1, v_rows, tm),
                                   lambda i: (i // per_key_tile, 0, i % per_key_tile))]
        out_shape += [jax.ShapeDtypeStruct((s, qk_w), BF16),
                      jax.ShapeDtypeStruct((s // ATTN_TILE, v_rows, ATTN_TILE), BF16)]
    if emit_q:
        operands += list(q_w)
        in_specs += [_resident(a.shape) for a in q_w]
        mem_w = q_w[1].shape[1] - qk_w
        out_specs += [pl.BlockSpec((qk_w, tm), lambda i: (0, i)), _rows(mem_w, tm)]
        out_shape += [jax.ShapeDtypeStruct((qk_w, s), BF16), jax.ShapeDtypeStruct((s, mem_w), F32)]
    return _row_call(functools.partial(_mlp_kernel, emit_kv=emit_kv, emit_q=emit_q), "mlp",
                     s, tm, operands, in_specs, out_specs, out_shape)


def _emit_kv(x, rope_ref, g_ref, w_ref, kgain_ref, kk_ref, vt_ref):
    gmat = _group_matrix()
    tm = x.shape[0]
    qk_w = kk_ref.shape[1]
    h = _rms(x, g_ref[...]).astype(BF16)
    kv = jnp.dot(h, w_ref[...], preferred_element_type=F32)
    cos_t, sin_lo, sin_hi = rope_ref[0], rope_ref[1], rope_ref[2]
    for g in range(qk_w // LANES):
        sl = slice(g * LANES, (g + 1) * LANES)
        kn = _head_rms(kv[:, sl], kgain_ref[...], gmat)
        kk_ref[:, sl] = _rope(kn, cos_t, sin_lo, sin_hi).astype(BF16)
    for g in range((kv.shape[1] - qk_w) // LANES):
        r0 = g * VT_ROWS
        vt_ref[0, r0:r0 + LANES, :] = kv[:, qk_w + g * LANES: qk_w + (g + 1) * LANES].T.astype(BF16)
        vt_ref[0, r0 + LANES:r0 + VT_ROWS, :] = jnp.ones((ONES_ROWS, tm), BF16)


def _emit_q(x, rope_ref, g_ref, w_ref, qgain_ref, qt_ref, qm_ref):
    gmat = _group_matrix()
    qk_w = qt_ref.shape[0]
    h = _rms(x, g_ref[...]).astype(BF16)
    proj = jnp.dot(h, w_ref[...], preferred_element_type=F32)
    cos_t, sin_lo, sin_hi = rope_ref[0], rope_ref[1], rope_ref[2]
    scale = HEAD_DIM ** -0.5 * LOG2E
    for g in range(qk_w // LANES):
        sl = slice(g * LANES, (g + 1) * LANES)
        qn = _head_rms(proj[:, sl], qgain_ref[...], gmat)
        qt_ref[sl, :] = (_rope(qn, cos_t, sin_lo, sin_hi) * scale).T.astype(BF16)
    qm_ref[...] = proj[:, qk_w:]


def _diff_attn_kernel(lam_ref, subln_ref, qt_ref, kk_ref, vt_ref, o_ref,
                      q2_scr, m_scr, acc_scr, s0_scr, mx0_scr, s1_scr, mx1_scr, *, lam_init):
    t = qt_ref.shape[1]
    qi = pl.program_id(1)
    qt = qt_ref[...]
    row = lax.broadcasted_iota(jnp.int32, qt.shape, 0)
    q2_scr[:, 0:t] = jnp.where(row < HEAD_DIM, qt, jnp.zeros_like(qt))
    q2_scr[:, t:2 * t] = jnp.where(row >= HEAD_DIM, qt, jnp.zeros_like(qt))
    m_scr[...] = jnp.full(m_scr.shape, -jnp.inf, F32)
    acc_scr[...] = jnp.zeros(acc_scr.shape, F32)

    def produce(j, buf):
        s_scr, mx_scr = buf
        k = kk_ref[pl.ds(pl.multiple_of(j * t, t), t), :]
        st = jnp.dot(k, q2_scr[...], preferred_element_type=F32)
        s_scr[...] = st
        mx_scr[...] = jnp.max(st, axis=0, keepdims=True)

    def consume(j, buf, diagonal):
        s_scr, mx_scr = buf
        st = s_scr[...]
        if diagonal:
            key = lax.broadcasted_iota(jnp.int32, st.shape, 0)
            qry = lax.broadcasted_iota(jnp.int32, st.shape, 1)
            st = jnp.where((key // CHUNK) <= ((qry % t) // CHUNK), st, -jnp.inf)
            mx = jnp.max(st, axis=0, keepdims=True)
        else:
            mx = mx_scr[...]
        m_prev = m_scr[...]
        m_new = jnp.maximum(m_prev, mx)
        alpha = jnp.exp2(m_prev - m_new)
        pt = jnp.exp2(st - m_new)
        acc_scr[...] = alpha * acc_scr[...] + jnp.dot(vt_ref[j], pt.astype(BF16),
                                                      preferred_element_type=F32)
        m_scr[...] = m_new

    buf0, buf1 = (s0_scr, mx0_scr), (s1_scr, mx1_scr)
    produce(0, buf0)

    def pair(p, carry):
        a = 2 * p
        produce(a + 1, buf1)
        consume(a, buf0, False)
        produce(a + 2, buf0)
        consume(a + 1, buf1, False)
        return carry

    lax.fori_loop(0, qi // 2, pair, 0)

    @pl.when(qi % 2 == 0)
    def _():
        consume(qi, buf0, True)

    @pl.when(qi % 2 == 1)
    def _():
        produce(qi, buf1)
        consume(qi - 1, buf0, False)
        consume(qi, buf1, True)

    lp = lam_ref[...]
    lam = (jnp.exp(jnp.sum(lp[0:1, :] * lp[1:2, :], axis=1, keepdims=True))
           - jnp.exp(jnp.sum(lp[2:3, :] * lp[3:4, :], axis=1, keepdims=True)) + lam_init)
    inv_l = 1.0 / acc_scr[LANES:LANES + 1, :]
    ot = (acc_scr[0:LANES, 0:t] * inv_l[:, 0:t]
          - lam * (acc_scr[0:LANES, t:2 * t] * inv_l[:, t:2 * t]))
    ot = ot * lax.rsqrt(jnp.mean(ot * ot, axis=0, keepdims=True) + SUBLN_EPS)
    o_ref[...] = (ot.T * subln_ref[...] * (1.0 - lam_init)).astype(BF16)


def _diff_attn(lam_p, subln, qt, kk, vt, lam_init):
    qk_w, s = qt.shape
    heads = qk_w // LANES
    t = ATTN_TILE
    return pl.pallas_call(
        functools.partial(_diff_attn_kernel, lam_init=lam_init),
        grid=(heads, s // t),
        in_specs=[
            pl.BlockSpec(lam_p.shape, lambda h, i: (0, 0)),
            pl.BlockSpec(subln.shape, lambda h, i: (0, 0)),
            pl.BlockSpec((LANES, t), lambda h, i: (h, i)),
            pl.BlockSpec((s, LANES), lambda h, i: (0, h)),
            pl.BlockSpec((s // t, VT_ROWS, t), lambda h, i: (0, h, 0)),
        ],
        out_specs=pl.BlockSpec((t, LANES), lambda h, i: (i, h)),
        out_shape=jax.ShapeDtypeStruct((s, heads * LANES), BF16),
        scratch_shapes=[pltpu.VMEM((LANES, 2 * t), BF16),
                        pltpu.VMEM((1, 2 * t), F32),
                        pltpu.VMEM((VT_ROWS, 2 * t), F32),
                        pltpu.VMEM((t, 2 * t), F32), pltpu.VMEM((1, 2 * t), F32),
                        pltpu.VMEM((t, 2 * t), F32), pltpu.VMEM((1, 2 * t), F32)],
        compiler_params=pltpu.CompilerParams(dimension_semantics=("arbitrary", "arbitrary"),
                                             vmem_limit_bytes=VMEM_LIMIT),
        name="diff_attn",
    )(lam_p, subln, qt, kk, vt)


def _rope_tables(seq):
    half = ROPE_DIM // 2
    inv = 1.0 / (ROPE_THETA ** (jnp.arange(0, ROPE_DIM, 2, dtype=F32) / ROPE_DIM))
    ang = jnp.arange(seq, dtype=F32)[:, None] * inv[None, :]
    cos, sin = jnp.cos(ang), jnp.sin(ang)
    ones = jnp.ones((seq, HEAD_DIM - ROPE_DIM), F32)
    zeros_half = jnp.zeros((seq, half), F32)
    zeros_rest = jnp.zeros((seq, HEAD_DIM - ROPE_DIM), F32)
    cos_t = jnp.concatenate([cos, cos, ones], axis=1)
    sin_lo = jnp.concatenate([-sin, zeros_half, zeros_rest], axis=1)
    sin_hi = jnp.concatenate([zeros_half, sin, zeros_rest], axis=1)
    reps = LANES // HEAD_DIM
    return jnp.stack([jnp.tile(t, (1, reps)) for t in (cos_t, sin_lo, sin_hi)])


def _pair_heads(w, qk_w):
    d = w.shape[0]
    heads = qk_w // HEAD_DIM
    m1 = w[:, :qk_w].reshape(d, heads, HEAD_DIM)
    m2 = w[:, qk_w:2 * qk_w].reshape(d, heads, HEAD_DIM)
    return jnp.concatenate([jnp.stack([m1, m2], axis=2).reshape(d, 2 * qk_w), w[:, 2 * qk_w:]], axis=1)


def kernel(x, mem, norm_mix, norm_mlp, a_w_in, a_conv, b_w_q, b_q_norm, b_lam, b_subln, kv_norm,
           w_kv, k_norm, mem_norm, w_mem_kv, mem_q_norm, mem_k_norm, w_o, w_up, w_down):
    b, s, d = x.shape
    assert b == 1
    depth = norm_mix.shape[0]
    n_a = a_w_in.shape[0]
    mem_w = w_mem_kv.shape[2] // 2
    qk_w = (b_w_q.shape[2] - mem_w) // 2
    mem_heads = mem_w // HEAD_DIM
    pair = LANES // HEAD_DIM

    xs = x[0]
    rope = _rope_tables(s)
    mk, mv = _mem_prep(mem[0], mem_norm[None, :], w_mem_kv,
                       jnp.tile(mem_k_norm, (1, mem_heads))[:, None, :])
    mem_qgain = jnp.tile(mem_q_norm, (1, mem_heads))
    w_in_b, w_o_b, w_up_b, w_down_b = (w.astype(BF16) for w in (a_w_in, w_o, w_up, w_down))
    kk = vt = qt = qm = None
    for l in range(depth):
        wo_l = _LayerOf(w_o_b, l)
        if l < n_a:
            xs = _a_mixer(xs, norm_mix[l][None, :], _LayerOf(w_in_b, l), a_conv[l],
                          mem_qgain[l][None, :], mk[l], mv[l], wo_l)
        else:
            j = l - n_a
            lam_init = 0.8 - 0.6 * math.exp(-0.3 * l)
            main = _diff_attn(b_lam[j], b_subln[j][None, :], qt, kk, vt, lam_init)
            xs = _b_out(xs, main, qm, mem_qgain[l][None, :], mk[l], mv[l], wo_l)
        kv_w = q_w = None
        if l == n_a - 1:
            kv_w = (kv_norm[None, :], _pair_heads(w_kv, qk_w).astype(BF16),
                    jnp.tile(k_norm, pair)[None, :])
        if n_a <= l + 1 < depth:
            j = l + 1 - n_a
            q_w = (norm_mix[l + 1][None, :], _pair_heads(b_w_q[j], qk_w).astype(BF16),
                   jnp.tile(b_q_norm[j], pair)[None, :])
        outs = _mlp(xs, norm_mlp[l][None, :], _LayerOf(w_up_b, l), _LayerOf(w_down_b, l),
                    rope, kv_w, q_w, 2 * qk_w)
        xs = outs[0]
        if kv_w is not None:
            kk, vt = outs[1], outs[2]
        if q_w is not None:
            qt, qm = outs[-2], outs[-1]
    return xs[None]
```
